```python
import jax, jax.numpy as jnp
from jax import lax
import numpy as np

D_MODEL = 1024
BATCH = 4
SEQ = 8192
DEPTH = 2
DEC_BATCH = 32
DEC_SEQ = 64
PAST_LEN = 2048

CHUNK = 64
PLE_DIM = 256
FFN_DIM = 2816
SB_HEADS = 8
SB_HEAD_DIM = 64
SB_WIDTH = SB_HEADS * SB_HEAD_DIM
SB_BLOCK = 128
GLA_HEADS = 4
GLA_KEY_DIM = 32
GLA_VAL_DIM = 64
GLA_KEY_WIDTH = GLA_HEADS * GLA_KEY_DIM
GLA_VAL_WIDTH = GLA_HEADS * GLA_VAL_DIM
GLA_GATE_RANK = 16
GLA_GATE_TAU = 16.0
POOL_WINDOWS = (2, 4, 8, 16)
POOL_GROUP_DIM = 64
POOL_WIDTH = len(POOL_WINDOWS) * POOL_GROUP_DIM
POOL_HIST = 15
N_BRANCH = 3
IN_SPLITS = (SB_WIDTH, SB_WIDTH, SB_WIDTH, GLA_KEY_WIDTH, GLA_KEY_WIDTH, GLA_VAL_WIDTH, GLA_GATE_RANK, GLA_VAL_WIDTH, POOL_WIDTH, N_BRANCH * D_MODEL)
IN_WIDTH = sum(IN_SPLITS)
RMS_EPS = 1e-6

kernel_name = 'hybrid_streaming_encoder_step'


def rms_norm(x, g):
    xf = x.astype(jnp.float32)
    y = xf * lax.rsqrt(jnp.mean(xf * xf, axis=-1, keepdims=True) + RMS_EPS)
    return (y * g.astype(jnp.float32)).astype(x.dtype)


def swiglu(x, w_in, w_out):
    a, b = jnp.split(x @ w_in, 2, axis=-1)
    return (jax.nn.silu(a) * b) @ w_out


def stick_breaking(q, k, v, n_past):
    B, T, H, d = q.shape
    qb = min(SB_BLOCK, T)
    nb = T // qb
    scale = SB_HEAD_DIM ** -0.5
    q_blocks = q.reshape(B, nb, qb, H, d).transpose(1, 0, 3, 2, 4)
    pos_blocks = (n_past + jnp.arange(T)).reshape(nb, qb)
    kh = k.transpose(0, 2, 1, 3)
    vh = v.transpose(0, 2, 1, 3)
    k_pos = jnp.arange(k.shape[1])

    def block(args):
        qblk, qpos = args
        z = jnp.einsum('bhqd,bhkd->bhqk', qblk, kh, preferred_element_type=jnp.float32) * scale
        mask = k_pos[None, :] < qpos[:, None]
        log_keep = jnp.where(mask, -jax.nn.softplus(z), 0.0)
        log_w = jax.nn.log_sigmoid(z) + lax.cumsum(log_keep, axis=3, reverse=True) - log_keep
        w = jnp.where(mask, jnp.exp(log_w), 0.0)
        return jnp.einsum('bhqk,bhkd->bhqd', w.astype(vh.dtype), vh)

    out = lax.map(block, (q_blocks, pos_blocks))
    return out.transpose(1, 0, 3, 2, 4).reshape(B, T, H * d)


def gla(q, k, v, log_alpha, s0):
    B, T, H, _ = q.shape
    c = min(CHUNK, T)
    n = T // c

    def chunks(a):
        return a.astype(jnp.float32).reshape(B, n, c, H, a.shape[-1]).transpose(1, 0, 3, 2, 4)

    causal = jnp.tril(jnp.ones((c, c), dtype=bool))

    def step(s, inp):
        qc, kc, vc, gc = inp
        b = jnp.cumsum(gc, axis=2)
        inter = jnp.einsum('bhtd,bhde->bhte', qc * jnp.exp(b), s)
        diff = b[:, :, :, None, :] - b[:, :, None, :, :]
        decay = jnp.exp(jnp.where(causal[:, :, None], diff, -jnp.inf))
        scores = jnp.einsum('bhtd,bhsd,bhtsd->bhts', qc, kc, decay)
        intra = jnp.einsum('bhts,bhse->bhte', scores, vc)
        b_last = b[:, :, -1:, :]
        s_new = jnp.exp(b_last[:, :, 0, :])[..., None] * s + jnp.einsum('bhsd,bhse->bhde', kc * jnp.exp(b_last - b), vc)
        return s_new, inter + intra

    s_fin, o = lax.scan(step, s0.astype(jnp.float32), (chunks(q), chunks(k), chunks(v), chunks(log_alpha)))
    return o.transpose(1, 0, 3, 2, 4).reshape(B, T, H, -1), s_fin


def pool_mixer(u, hist, n_past, pool_w, pool_scale):
    B, T, _ = u.shape
    ext = jnp.concatenate([hist.astype(jnp.float32), u.astype(jnp.float32)], axis=1)
    cs = jnp.concatenate([jnp.zeros((B, 1, POOL_WIDTH), jnp.float32), jnp.cumsum(ext, axis=1)], axis=1)
    pos = n_past + jnp.arange(T)
    uf = u.astype(jnp.float32)
    outs = []
    for gi, w in enumerate(POOL_WINDOWS):
        lo, hi = gi * POOL_GROUP_DIM, (gi + 1) * POOL_GROUP_DIM
        wsum = cs[:, POOL_HIST + 1:, lo:hi] - cs[:, POOL_HIST + 1 - w:POOL_HIST + 1 - w + T, lo:hi]
        cnt = jnp.minimum(w, pos + 1).astype(jnp.float32)[None, :, None]
        outs.append(wsum / cnt - uf[..., lo:hi])
    d = jnp.stack(outs, axis=2)
    y = jnp.einsum('btgc,gcd->btgd', d, pool_w.astype(jnp.float32)).reshape(B, T, POOL_WIDTH)
    y = y * pool_scale.astype(jnp.float32)
    return y.astype(u.dtype), ext[:, -POOL_HIST:]


def token_mixer(h, k_cache, v_cache, s0, pool_hist, w):
    B, T, _ = h.shape
    n_past = k_cache.shape[1]
    proj = h @ w['w_in']
    parts = []
    off = 0
    for size in IN_SPLITS:
        parts.append(proj[..., off:off + size])
        off += size
    q_a, k_a, v_a, q_b, k_b, v_b, r_b, o_b, u_c, gate_logits = parts

    k_a = k_a.reshape(B, T, SB_HEADS, SB_HEAD_DIM).astype(k_cache.dtype)
    v_a = v_a.reshape(B, T, SB_HEADS, SB_HEAD_DIM).astype(v_cache.dtype)
    k_all = jnp.concatenate([k_cache, k_a], axis=1)
    v_all = jnp.concatenate([v_cache, v_a], axis=1)
    y_a = stick_breaking(q_a.reshape(B, T, SB_HEADS, SB_HEAD_DIM), k_all, v_all, n_past).astype(h.dtype)

    log_alpha = jax.nn.log_sigmoid((r_b @ w['gla_w_gate'] + w['gla_b_gate']).astype(jnp.float32)) / GLA_GATE_TAU
    o, s_new = gla(q_b.reshape(B, T, GLA_HEADS, GLA_KEY_DIM) * (GLA_KEY_DIM ** -0.5),
                   k_b.reshape(B, T, GLA_HEADS, GLA_KEY_DIM),
                   v_b.reshape(B, T, GLA_HEADS, GLA_VAL_DIM),
                   log_alpha.reshape(B, T, GLA_HEADS, GLA_KEY_DIM), s0)
    o = rms_norm(o, w['gla_norm'].reshape(GLA_HEADS, GLA_VAL_DIM)).reshape(B, T, GLA_VAL_WIDTH)
    y_b = o.astype(h.dtype) * jax.nn.silu(o_b)

    y_c, pool_new = pool_mixer(u_c, pool_hist, n_past, w['pool_w'], w['pool_scale'])

    gates = jax.nn.sigmoid(gate_logits.astype(jnp.float32)).astype(h.dtype).reshape(B, T, N_BRANCH, D_MODEL)
    merged = (gates[:, :, 0] * (y_a @ w['w_branch_a'])
              + gates[:, :, 1] * (y_b @ w['w_branch_b'])
              + gates[:, :, 2] * (y_c @ w['w_branch_c']))
    return merged @ w['w_out'], k_a, v_a, s_new.astype(s0.dtype), pool_new.astype(pool_hist.dtype)


def run_trunk(x, p, k_cache, v_cache, s_gla, s_pool, weights):
    ks, vs, ss, ps = [], [], [], []
    for i in range(DEPTH):
        w = {name: arr[i] for name, arr in weights.items()}
        x = x + 0.5 * swiglu(rms_norm(x, w['ffn1_norm']), w['ffn1_w_in'], w['ffn1_w_out'])
        mix, k_new, v_new, s_new, pool_new = token_mixer(rms_norm(x, w['mix_norm']), k_cache[i], v_cache[i], s_gla[i], s_pool[i], w)
        x = x + mix
        x = x + 0.5 * swiglu(rms_norm(x, w['ffn2_norm']), w['ffn2_w_in'], w['ffn2_w_out'])
        x = x + jax.nn.sigmoid(rms_norm(x, w['ple_norm']) @ w['ple_w_gate']) * (p[i] @ w['ple_w_proj'])
        ks.append(k_new)
        vs.append(v_new)
        ss.append(s_new)
        ps.append(pool_new)
    return x, jnp.stack(ks), jnp.stack(vs), jnp.stack(ss), jnp.stack(ps)


def setup_inputs(seed: int = 0) -> dict:
    key = jax.random.key(seed)
    ks = jax.random.split(key, 32)

    def normal(k, shape, scale):
        return jax.random.normal(k, shape, jnp.float32) * scale

    return {
        'x_prompt': normal(ks[0], (BATCH, SEQ, D_MODEL), 1.0),
        'x_sample': normal(ks[1], (DEC_BATCH, DEC_SEQ, D_MODEL), 1.0),
        'cache_sb_k': normal(ks[2], (DEPTH, DEC_BATCH, PAST_LEN, SB_HEADS, SB_HEAD_DIM), 1.0),
        'cache_sb_v': normal(ks[3], (DEPTH, DEC_BATCH, PAST_LEN, SB_HEADS, SB_HEAD_DIM), 1.0),
        'state_gla': normal(ks[4], (DEPTH, DEC_BATCH, GLA_HEADS, GLA_KEY_DIM, GLA_VAL_DIM), 1.0),
        'state_pool': normal(ks[5], (DEPTH, DEC_BATCH, POOL_HIST, POOL_WIDTH), 1.0),
        'p_prompt': normal(ks[6], (DEPTH, BATCH, SEQ, PLE_DIM), 1.0),
        'p_sample': normal(ks[7], (DEPTH, DEC_BATCH, DEC_SEQ, PLE_DIM), 1.0),
        'ffn1_norm': 1.0 + normal(ks[8], (DEPTH, D_MODEL), 0.02),
        'ffn1_w_in': normal(ks[9], (DEPTH, D_MODEL, 2 * FFN_DIM), D_MODEL ** -0.5),
        'ffn1_w_out': normal(ks[10], (DEPTH, FFN_DIM, D_MODEL), FFN_DIM ** -0.5),
        'mix_norm': 1.0 + normal(ks[11], (DEPTH, D_MODEL), 0.02),
        'w_in': normal(ks[12], (DEPTH, D_MODEL, IN_WIDTH), D_MODEL ** -0.5),
        'gla_w_gate': normal(ks[13], (DEPTH, GLA_GATE_RANK, GLA_KEY_WIDTH), GLA_GATE_RANK ** -0.5),
        'gla_b_gate': normal(ks[14], (DEPTH, GLA_KEY_WIDTH), 0.01),
        'gla_norm': 1.0 + normal(ks[15], (DEPTH, GLA_VAL_WIDTH), 0.02),
        'pool_w': normal(ks[16], (DEPTH, len(POOL_WINDOWS), POOL_GROUP_DIM, POOL_GROUP_DIM), POOL_GROUP_DIM ** -0.5),
        'pool_scale': 1.0 + normal(ks[17], (DEPTH, POOL_WIDTH), 0.02),
        'w_branch_a': normal(ks[18], (DEPTH, SB_WIDTH, D_MODEL), SB_WIDTH ** -0.5),
        'w_branch_b': normal(ks[19], (DEPTH, GLA_VAL_WIDTH, D_MODEL), GLA_VAL_WIDTH ** -0.5),
        'w_branch_c': normal(ks[20], (DEPTH, POOL_WIDTH, D_MODEL), POOL_WIDTH ** -0.5),
        'w_out': normal(ks[21], (DEPTH, D_MODEL, D_MODEL), D_MODEL ** -0.5),
        'ffn2_norm': 1.0 + normal(ks[22], (DEPTH, D_MODEL), 0.02),
        'ffn2_w_in': normal(ks[23], (DEPTH, D_MODEL, 2 * FFN_DIM), D_MODEL ** -0.5),
        'ffn2_w_out': normal(ks[24], (DEPTH, FFN_DIM, D_MODEL), FFN_DIM ** -0.5),
        'ple_norm': 1.0 + normal(ks[25], (DEPTH, D_MODEL), 0.02),
        'ple_w_gate': normal(ks[26], (DEPTH, D_MODEL, D_MODEL), D_MODEL ** -0.5),
        'ple_w_proj': normal(ks[27], (DEPTH, PLE_DIM, D_MODEL), PLE_DIM ** -0.5),
        'final_norm': 1.0 + normal(ks[28], (D_MODEL,), 0.02),
    }


def reference(x_prompt, x_sample, cache_sb_k, cache_sb_v, state_gla, state_pool, p_prompt, p_sample,
              ffn1_norm, ffn1_w_in, ffn1_w_out, mix_norm, w_in, gla_w_gate, gla_b_gate, gla_norm,
              pool_w, pool_scale, w_branch_a, w_branch_b, w_branch_c, w_out,
              ffn2_norm, ffn2_w_in, ffn2_w_out, ple_norm, ple_w_gate, ple_w_proj, final_norm):
    weights = dict(ffn1_norm=ffn1_norm, ffn1_w_in=ffn1_w_in, ffn1_w_out=ffn1_w_out, mix_norm=mix_norm,
                   w_in=w_in, gla_w_gate=gla_w_gate, gla_b_gate=gla_b_gate, gla_norm=gla_norm,
                   pool_w=pool_w, pool_scale=pool_scale, w_branch_a=w_branch_a, w_branch_b=w_branch_b,
                   w_branch_c=w_branch_c, w_out=w_out, ffn2_norm=ffn2_norm, ffn2_w_in=ffn2_w_in,
                   ffn2_w_out=ffn2_w_out, ple_norm=ple_norm, ple_w_gate=ple_w_gate, ple_w_proj=ple_w_proj)
    b_prompt = x_prompt.shape[0]
    empty_kv = jnp.zeros((DEPTH, b_prompt, 0, SB_HEADS, SB_HEAD_DIM), cache_sb_k.dtype)
    zero_gla = jnp.zeros((DEPTH, b_prompt, GLA_HEADS, GLA_KEY_DIM, GLA_VAL_DIM), state_gla.dtype)
    zero_pool = jnp.zeros((DEPTH, b_prompt, POOL_HIST, POOL_WIDTH), state_pool.dtype)
    h_prompt, sb_k_prompt, sb_v_prompt, gla_state_prompt, pool_state_prompt = run_trunk(
        x_prompt, p_prompt, empty_kv, empty_kv, zero_gla, zero_pool, weights)
    h_sample, sb_k_sample, sb_v_sample, gla_state_sample, pool_state_sample = run_trunk(
        x_sample, p_sample, cache_sb_k, cache_sb_v, state_gla, state_pool, weights)
    y_prompt = rms_norm(h_prompt, final_norm)
    y_sample = rms_norm(h_sample, final_norm)
    return (y_prompt, y_sample, sb_k_prompt, sb_v_prompt, gla_state_prompt, pool_state_prompt,
            sb_k_sample, sb_v_sample, gla_state_sample, pool_state_sample)
```

```python
import functools

import jax
import jax.numpy as jnp
from jax import lax
from jax.experimental import pallas as pl
from jax.experimental.pallas import tpu as pltpu

D_MODEL = 1024
FFN_DIM = 2816
PLE_DIM = 256
SB_HEADS = 8
SB_HEAD_DIM = 64
SB_WIDTH = SB_HEADS * SB_HEAD_DIM
GLA_HEADS = 4
GLA_KEY_DIM = 32
GLA_VAL_DIM = 64
GLA_KEY_WIDTH = GLA_HEADS * GLA_KEY_DIM
GLA_VAL_WIDTH = GLA_HEADS * GLA_VAL_DIM
GLA_GATE_RANK = 16
GLA_GATE_TAU = 16.0
GLA_CHUNK = 64
POOL_WINDOWS = (2, 4, 8, 16)
POOL_GROUP_DIM = 64
POOL_WIDTH = len(POOL_WINDOWS) * POOL_GROUP_DIM
POOL_HIST = 15
POOL_HIST_ROWS = 16
RMS_EPS = 1e-6

LANES = 128
VMEM_LIMIT_BYTES = 56 * 1024 * 1024
TOKEN_TILE = 512
FFN_CHUNK = 256
SB_QUERY_BLOCK = 128
SB_KEY_BLOCK = 128
SB_LOG_ZERO = -104.0

F32 = jnp.float32
BF16 = jnp.bfloat16


def _dot(a, b):
    return jnp.dot(a, b, preferred_element_type=F32)


def _dot_nt(a, b):
    return lax.dot_general(a, b, (((1,), (1,)), ((), ())), preferred_element_type=F32)


def _dot_tn(a, b):
    return lax.dot_general(a, b, (((0,), (0,)), ((), ())), preferred_element_type=F32)


def _dot_split(a, b):
    hi = a.astype(BF16)
    lo = (a - hi.astype(F32)).astype(BF16)
    return _dot(hi, b) + _dot(lo, b)


def _iota_div(shape, axis, size):
    assert size & (size - 1) == 0
    return lax.broadcasted_iota(jnp.int32, shape, axis) >> (size.bit_length() - 1)


def _rms(x, g):
    ms = jnp.mean(x * x, axis=-1, keepdims=True)
    return x * lax.rsqrt(ms + RMS_EPS) * g


def _softplus_parts(z):
    t = jnp.log1p(jnp.exp(-jnp.abs(z)))
    return jnp.maximum(z, 0.0) + t, jnp.minimum(z, 0.0) - t


def _params(*semantics):
    return pltpu.CompilerParams(dimension_semantics=semantics, vmem_limit_bytes=VMEM_LIMIT_BYTES)


def _const_spec(shape):
    zeros = (0,) * len(shape)
    return pl.BlockSpec(shape, lambda *_: zeros, pipeline_mode=pl.Buffered(1))


def _swiglu_update(x, norm_ref, wa_ref, wb_ref, wo_ref):
    h = _rms(x, norm_ref[...]).astype(BF16)
    acc = jnp.zeros(x.shape, F32)
    for f in range(0, FFN_DIM, FFN_CHUNK):
        a = _dot(h, wa_ref[:, f:f + FFN_CHUNK])
        b = _dot(h, wb_ref[:, f:f + FFN_CHUNK])
        g = (a * jax.nn.sigmoid(a) * b).astype(BF16)
        acc = acc + _dot(g, wo_ref[f:f + FFN_CHUNK, :])
    return x + 0.5 * acc


def _ffn_kernel(x_ref, norm_ref, wa_ref, wb_ref, wo_ref, o_ref):
    o_ref[...] = _swiglu_update(x_ref[...], norm_ref, wa_ref, wb_ref, wo_ref)


def _ffn_ple_kernel(x_ref, p_ref, norm_ref, wa_ref, wb_ref, wo_ref, pnorm_ref, wpg_ref, wpp_ref, fnorm_ref,
                    o_ref, *, final):
    x = _swiglu_update(x_ref[...], norm_ref, wa_ref, wb_ref, wo_ref)
    gate = jax.nn.sigmoid(_dot(_rms(x, pnorm_ref[...]).astype(BF16), wpg_ref[...]))
    x = x + gate * _dot(p_ref[...].astype(BF16), wpp_ref[...])
    if final:
        x = _rms(x, fnorm_ref[...])
    o_ref[...] = x


def _ffn(x, w):
    n = x.shape[0]
    tok = lambda i: (i, 0)
    return pl.pallas_call(
        _ffn_kernel,
        grid=(n // TOKEN_TILE,),
        in_specs=[pl.BlockSpec((TOKEN_TILE, D_MODEL), tok),
                  _const_spec((1, D_MODEL)),
                  _const_spec((D_MODEL, FFN_DIM)),
                  _const_spec((D_MODEL, FFN_DIM)),
                  _const_spec((FFN_DIM, D_MODEL))],
        out_specs=pl.BlockSpec((TOKEN_TILE, D_MODEL), tok),
        out_shape=jax.ShapeDtypeStruct((n, D_MODEL), F32),
        compiler_params=_params("parallel"),
        name="ffn",
    )(x, w["ffn1_norm"], w["ffn1_wa"], w["ffn1_wb"], w["ffn1_wo"])


def _ffn_ple(x, p, w, final_norm, final):
    n = x.shape[0]
    tok = lambda i: (i, 0)
    return pl.pallas_call(
        functools.partial(_ffn_ple_kernel, final=final),
        grid=(n // TOKEN_TILE,),
        in_specs=[pl.BlockSpec((TOKEN_TILE, D_MODEL), tok),
                  pl.BlockSpec((TOKEN_TILE, PLE_DIM), tok),
                  _const_spec((1, D_MODEL)),
                  _const_spec((D_MODEL, FFN_DIM)),
                  _const_spec((D_MODEL, FFN_DIM)),
                  _const_spec((FFN_DIM, D_MODEL)),
                  _const_spec((1, D_MODEL)),
                  _const_spec((D_MODEL, D_MODEL)),
                  _const_spec((PLE_DIM, D_MODEL)),
                  _const_spec((1, D_MODEL))],
        out_specs=pl.BlockSpec((TOKEN_TILE, D_MODEL), tok),
        out_shape=jax.ShapeDtypeStruct((n, D_MODEL), F32),
        compiler_params=_params("parallel"),
        name="ffn_ple",
    )(x, p, w["ffn2_norm"], w["ffn2_wa"], w["ffn2_wb"], w["ffn2_wo"], w["ple_norm"], w["ple_wg"], w["ple_wp"],
      final_norm)


def _qkv_kernel(x_ref, norm_ref, w_ref, q_ref, k_ref, v_ref):
    h = _rms(x_ref[...], norm_ref[...]).astype(BF16)
    u = _dot(h, w_ref[...])
    q_ref[...] = (u[:, :SB_WIDTH] * SB_HEAD_DIM ** -0.5).astype(BF16)
    k_ref[...] = u[:, SB_WIDTH:2 * SB_WIDTH]
    v_ref[...] = u[:, 2 * SB_WIDTH:]


def _qkv(x, w):
    n = x.shape[0]
    tok = lambda i: (i, 0)
    return pl.pallas_call(
        _qkv_kernel,
        grid=(n // TOKEN_TILE,),
        in_specs=[pl.BlockSpec((TOKEN_TILE, D_MODEL), tok),
                  _const_spec((1, D_MODEL)),
                  _const_spec((D_MODEL, 3 * SB_WIDTH))],
        out_specs=[pl.BlockSpec((TOKEN_TILE, SB_WIDTH), tok)] * 3,
        out_shape=[jax.ShapeDtypeStruct((n, SB_WIDTH), BF16),
                   jax.ShapeDtypeStruct((n, SB_WIDTH), F32),
                   jax.ShapeDtypeStruct((n, SB_WIDTH), F32)],
        compiler_params=_params("parallel"),
        name="qkv",
    )(x, w["mix_norm"], w["w_qkv"])


def _sb_kernel(*refs, qb, kb, has_cache):
    if has_cache:
        q_ref, kn_ref, vn_ref, kc_ref, vc_ref, o_ref, acc_ref, c_ref = refs
    else:
        q_ref, kn_ref, vn_ref, o_ref, acc_ref, c_ref = refs
        kc_ref, vc_ref = kn_ref, vn_ref
    i = pl.program_id(2)
    lane = lax.broadcasted_iota(jnp.int32, (1, LANES), 1)
    head_lanes = [lane < SB_HEAD_DIM, lane >= SB_HEAD_DIM]
    q = q_ref[0]
    q_heads = [jnp.where(m, q, jnp.zeros_like(q)) for m in head_lanes]

    def visit(k_blk, v_blk, mask):
        n = k_blk.shape[0]
        later = (lax.broadcasted_iota(jnp.int32, (n, n), 0) > lax.broadcasted_iota(jnp.int32, (n, n), 1)).astype(BF16)
        k_blk = k_blk.astype(BF16)
        v_blk = v_blk.astype(BF16)
        out = acc_ref[...]
        for e in range(2):
            z = _dot_nt(q_heads[e], k_blk)
            sp, ls = _softplus_parts(z)
            lk = -sp if mask is None else jnp.where(mask, -sp, 0.0)
            cs = _dot_split(lk, later)
            c = c_ref[e]
            w = jnp.exp(ls + cs + c[:, :n])
            if mask is not None:
                w = jnp.where(mask, w, 0.0)
            v_head = jnp.where(head_lanes[e], v_blk, jnp.zeros_like(v_blk))
            out = out + _dot(w.astype(BF16), v_head)
            c_ref[e] = c + (cs[:, 0:1] + lk[:, 0:1])
        acc_ref[...] = out

    acc_ref[...] = jnp.zeros_like(acc_ref)
    c_ref[...] = jnp.zeros_like(c_ref)
    start = pl.multiple_of(i * qb, qb)
    causal = lax.broadcasted_iota(jnp.int32, (qb, qb), 1) < lax.broadcasted_iota(jnp.int32, (qb, qb), 0)
    visit(kn_ref[0, pl.ds(start, qb), :], vn_ref[0, pl.ds(start, qb), :], causal)

    n_full = kc_ref.shape[1] // kb if has_cache else (i * qb) // kb

    def live(j):
        return jnp.logical_and(j >= 0, jnp.max(c_ref[...]) >= SB_LOG_ZERO)

    def body(j):
        rows = pl.ds(pl.multiple_of(j * kb, kb), kb)
        visit(kc_ref[0, rows, :], vc_ref[0, rows, :], None)
        return j - 1

    lax.while_loop(live, body, n_full - 1)
    o_ref[0] = acc_ref[...].astype(o_ref.dtype)


def _sb_attention(q, k_new, v_new, k_cache, v_cache):
    b, t, _ = q.shape
    has_cache = k_cache is not None
    qb = min(SB_QUERY_BLOCK, t)
    kb = SB_KEY_BLOCK
    if has_cache:
        assert t == qb and k_cache.shape[1] % kb == 0
    else:
        assert qb % kb == 0 and t % qb == 0
    pairs = SB_WIDTH // LANES
    blk = lambda bi, hp, i: (bi, i, hp)
    full = lambda bi, hp, i: (bi, 0, hp)
    in_specs = [pl.BlockSpec((1, qb, LANES), blk),
                pl.BlockSpec((1, t, LANES), full),
                pl.BlockSpec((1, t, LANES), full)]
    args = [q, k_new, v_new]
    if has_cache:
        past = k_cache.shape[1]
        in_specs += [pl.BlockSpec((1, past, LANES), full)] * 2
        args += [k_cache, v_cache]
    return pl.pallas_call(
        functools.partial(_sb_kernel, qb=qb, kb=kb, has_cache=has_cache),
        grid=(b, pairs, t // qb),
        in_specs=in_specs,
        out_specs=pl.BlockSpec((1, qb, LANES), blk),
        out_shape=jax.ShapeDtypeStruct((b, t, SB_WIDTH), BF16),
        scratch_shapes=[pltpu.VMEM((qb, LANES), F32), pltpu.VMEM((2, qb, LANES), F32)],
        compiler_params=_params("parallel", "parallel", "arbitrary"),
        name="sb_attention",
    )(*args)


def _gla_chunk(q, k, v, g, st, masks):
    tril, causal, k_heads, v_heads, st_heads = masks
    g_hi = g.astype(BF16)
    g_lo = (g - g_hi.astype(F32)).astype(BF16)
    b = _dot(tril, g_hi) + _dot(tril, g_lo)
    mid = b[GLA_CHUNK // 2 - 1:GLA_CHUNK // 2, :]
    last = b[GLA_CHUNK - 1:GLA_CHUNK, :]
    q_in = (q * jnp.exp(b - mid)).astype(BF16)
    k_in = k * jnp.exp(mid - b)
    k_rows = (jnp.concatenate([k_in] * GLA_HEADS, axis=0) * k_heads).astype(BF16)
    v_bf = v.astype(BF16)
    v_rows = jnp.concatenate([v_bf] * GLA_HEADS, axis=0) * v_heads
    scores = jnp.where(causal, _dot_nt(q_in, k_rows), 0.0).astype(BF16)
    intra = _dot(scores, v_rows)
    inter = _dot_nt((q * jnp.exp(b)).astype(BF16), st.astype(BF16))
    k_out = (k * jnp.exp(last - b)).astype(BF16)
    st_new = st * jnp.exp(last) + _dot_tn(v_bf, k_out) * st_heads
    return inter + intra, st_new


def _mixer_kernel(x_ref, ya_ref, st0_ref, hist0_ref, norm_ref, wp_ref, wgate_ref, glawg_ref, glabg_ref, glanorm_ref,
                  poolw_ref, poolscale_ref, wa_ref, wb_ref, wc_ref, wout_ref,
                  o_ref, st_ref, hist_ref, st_s, ext_s, *, seqs, rows, n_past):
    t = pl.program_id(1)
    m = seqs * rows

    @pl.when(t == 0)
    def _():
        st_s[...] = st0_ref[...]
        ext_s[:, 0:POOL_HIST_ROWS, :] = hist0_ref[...]

    x = x_ref[...].reshape(m, D_MODEL)
    h = _rms(x, norm_ref[...]).astype(BF16)
    u = _dot(h, wp_ref[...])
    kw, vw = GLA_KEY_WIDTH, GLA_VAL_WIDTH
    q_b = u[:, 0:kw] * GLA_KEY_DIM ** -0.5
    k_b = u[:, kw:2 * kw]
    v_b = u[:, 2 * kw:2 * kw + vw]
    r_b = u[:, 2 * kw + vw:3 * kw + vw]
    o_b = u[:, 3 * kw + vw:3 * kw + 2 * vw]
    u_c = u[:, 3 * kw + 2 * vw:]

    gate = _dot(r_b.astype(BF16), glawg_ref[...]) + glabg_ref[...]
    log_alpha = (jnp.minimum(gate, 0.0) - jnp.log1p(jnp.exp(-jnp.abs(gate)))) / GLA_GATE_TAU
    c = GLA_CHUNK
    ri = lax.broadcasted_iota(jnp.int32, (c, c), 0)
    ci = lax.broadcasted_iota(jnp.int32, (c, c), 1)
    tril = (ci <= ri).astype(BF16)
    key_head = _iota_div((1, kw), 1, GLA_KEY_DIM)
    val_head = _iota_div((1, vw), 1, GLA_VAL_DIM)
    rows_hs = _iota_div((GLA_HEADS * c, 1), 0, c)
    causal = (lax.broadcasted_iota(jnp.int32, (c, GLA_HEADS * c), 1) & (c - 1)) <= lax.broadcasted_iota(
        jnp.int32, (c, GLA_HEADS * c), 0)
    k_heads = (rows_hs == key_head).astype(F32)
    v_heads = (rows_hs == val_head).astype(BF16)
    st_heads = (_iota_div((vw, 1), 0, GLA_VAL_DIM) == key_head).astype(F32)
    masks = (tril, causal, k_heads, v_heads, st_heads)
    o_parts = []
    for s in range(seqs):
        st = st_s[s]
        for j in range(rows // c):
            r0 = s * rows + j * c
            o, st = _gla_chunk(q_b[r0:r0 + c], k_b[r0:r0 + c], v_b[r0:r0 + c], log_alpha[r0:r0 + c], st, masks)
            o_parts.append(o)
        st_s[s] = st
    o = jnp.concatenate(o_parts, axis=0)
    head_mean = (_iota_div((vw, 1), 0, GLA_VAL_DIM) == val_head).astype(BF16) * (1.0 / GLA_VAL_DIM)
    ms = _dot_split(o * o, head_mean)
    y_b = o * lax.rsqrt(ms + RMS_EPS) * glanorm_ref[...] * (o_b * jax.nn.sigmoid(o_b))

    lane = lax.broadcasted_iota(jnp.int32, (1, POOL_WIDTH), 1)
    pos1 = n_past + t * rows + 1 + lax.broadcasted_iota(jnp.int32, (rows, 1), 0)
    d_parts = []
    for s in range(seqs):
        ext_s[s, POOL_HIST_ROWS:, :] = u_c[s * rows:(s + 1) * rows]
        e = ext_s[s]
        sums = []
        span = 1
        for w in POOL_WINDOWS:
            while span < w:
                e = e + pltpu.roll(e, span, 0)
                span *= 2
            sums.append(e[POOL_HIST_ROWS:])
        wsum = sums[-1]
        cnt = jnp.minimum(POOL_WINDOWS[-1], pos1)
        for gi in range(len(POOL_WINDOWS) - 2, -1, -1):
            in_group = lane < (gi + 1) * POOL_GROUP_DIM
            wsum = jnp.where(in_group, sums[gi], wsum)
            cnt = jnp.where(in_group, jnp.minimum(POOL_WINDOWS[gi], pos1), cnt)
        d_parts.append(wsum / cnt.astype(F32) - u_c[s * rows:(s + 1) * rows])
        ext_s[s, 0:POOL_HIST_ROWS, :] = ext_s[s, rows:rows + POOL_HIST_ROWS, :]
    d = jnp.concatenate(d_parts, axis=0)
    y_c = _dot(d.astype(BF16), poolw_ref[...]) * poolscale_ref[...]

    branches = ((ya_ref[...].reshape(m, SB_WIDTH), wa_ref), (y_b.astype(BF16), wb_ref), (y_c.astype(BF16), wc_ref))
    merged = None
    for bi, (y, w_ref) in enumerate(branches):
        gate_b = jax.nn.sigmoid(_dot(h, wgate_ref[:, bi * D_MODEL:(bi + 1) * D_MODEL]))
        term = gate_b * _dot(y, w_ref[...])
        merged = term if merged is None else merged + term
    o_ref[...] = (x + _dot(merged.astype(BF16), wout_ref[...])).reshape(seqs, rows, D_MODEL)

    @pl.when(t == pl.num_programs(1) - 1)
    def _():
        st_ref[...] = st_s[...]
        hist_ref[...] = ext_s[:, 0:POOL_HIST_ROWS, :]


def _mixer(x, y_a, st0, hist0, w, n_past):
    b, t, _ = x.shape
    rows = min(TOKEN_TILE, t)
    seqs = TOKEN_TILE // rows
    assert t % rows == 0 and b % seqs == 0 and rows % GLA_CHUNK == 0
    tok = lambda bi, ti: (bi, ti, 0)
    per_seq = lambda bi, ti: (bi, 0, 0)
    kw, vw = GLA_KEY_WIDTH, GLA_VAL_WIDTH
    return pl.pallas_call(
        functools.partial(_mixer_kernel, seqs=seqs, rows=rows, n_past=n_past),
        grid=(b // seqs, t // rows),
        in_specs=[pl.BlockSpec((seqs, rows, D_MODEL), tok),
                  pl.BlockSpec((seqs, rows, SB_WIDTH), tok),
                  pl.BlockSpec((seqs, vw, kw), per_seq),
                  pl.BlockSpec((seqs, POOL_HIST_ROWS, POOL_WIDTH), per_seq),
                  _const_spec((1, D_MODEL)),
                  _const_spec((D_MODEL, 3 * kw + 3 * vw)),
                  _const_spec((D_MODEL, 3 * D_MODEL)),
                  _const_spec((kw, kw)),
                  _const_spec((1, kw)),
                  _const_spec((1, vw)),
                  _const_spec((POOL_WIDTH, POOL_WIDTH)),
                  _const_spec((1, POOL_WIDTH)),
                  _const_spec((SB_WIDTH, D_MODEL)),
                  _const_spec((vw, D_MODEL)),
                  _const_spec((POOL_WIDTH, D_MODEL)),
                  _const_spec((D_MODEL, D_MODEL))],
        out_specs=[pl.BlockSpec((seqs, rows, D_MODEL), tok),
                   pl.BlockSpec((seqs, vw, kw), per_seq),
                   pl.BlockSpec((seqs, POOL_HIST_ROWS, POOL_WIDTH), per_seq)],
        out_shape=[jax.ShapeDtypeStruct((b, t, D_MODEL), F32),
                   jax.ShapeDtypeStruct((b, vw, kw), F32),
                   jax.ShapeDtypeStruct((b, POOL_HIST_ROWS, POOL_WIDTH), F32)],
        scratch_shapes=[pltpu.VMEM((seqs, vw, kw), F32),
                        pltpu.VMEM((seqs, POOL_HIST_ROWS + rows, POOL_WIDTH), F32)],
        compiler_params=_params("parallel", "arbitrary"),
        name="mixer",
    )(x, y_a, st0, hist0, w["mix_norm"], w["w_proj"], w["w_gate"], w["gla_wg"], w["gla_bg"], w["gla_norm"],
      w["pool_w"], w["pool_scale"], w["w_branch_a"], w["w_branch_b"], w["w_branch_c"], w["w_out"])


def _layer_weights(i, p):
    row = lambda a: a[i].reshape(1, -1).astype(F32)
    bf = lambda a: a.astype(BF16)
    kw, vw = GLA_KEY_WIDTH, GLA_VAL_WIDTH
    w_in = p["w_in"][i]
    a_end = 3 * SB_WIDTH
    qkv_b = w_in[:, a_end:a_end + 2 * kw + vw]
    r0 = a_end + 2 * kw + vw
    r_b = jnp.pad(w_in[:, r0:r0 + GLA_GATE_RANK], ((0, 0), (0, kw - GLA_GATE_RANK)))
    o0 = r0 + GLA_GATE_RANK
    rest = w_in[:, o0:o0 + vw + POOL_WIDTH]
    g0 = o0 + vw + POOL_WIDTH
    pool_w = jnp.zeros((POOL_WIDTH, POOL_WIDTH), F32)
    for gi in range(len(POOL_WINDOWS)):
        lo = gi * POOL_GROUP_DIM
        pool_w = pool_w.at[lo:lo + POOL_GROUP_DIM, lo:lo + POOL_GROUP_DIM].set(p["pool_w"][i, gi])
    return dict(
        ffn1_norm=row(p["ffn1_norm"]), ffn1_wa=bf(p["ffn1_w_in"][i, :, :FFN_DIM]), ffn1_wb=bf(p["ffn1_w_in"][i, :, FFN_DIM:]),
        ffn1_wo=bf(p["ffn1_w_out"][i]),
        mix_norm=row(p["mix_norm"]), w_qkv=bf(w_in[:, :a_end]),
        w_proj=bf(jnp.concatenate([qkv_b, r_b, rest], axis=1)), w_gate=bf(w_in[:, g0:]),
        gla_wg=bf(jnp.pad(p["gla_w_gate"][i], ((0, kw - GLA_GATE_RANK), (0, 0)))),
        gla_bg=row(p["gla_b_gate"]), gla_norm=row(p["gla_norm"]),
        pool_w=bf(pool_w), pool_scale=row(p["pool_scale"]),
        w_branch_a=bf(p["w_branch_a"][i]), w_branch_b=bf(p["w_branch_b"][i]), w_branch_c=bf(p["w_branch_c"][i]),
        w_out=bf(p["w_out"][i]),
        ffn2_norm=row(p["ffn2_norm"]), ffn2_wa=bf(p["ffn2_w_in"][i, :, :FFN_DIM]), ffn2_wb=bf(p["ffn2_w_in"][i, :, FFN_DIM:]),
        ffn2_wo=bf(p["ffn2_w_out"][i]),
        ple_norm=row(p["ple_norm"]), ple_wg=bf(p["ple_w_gate"][i]), ple_wp=bf(p["ple_w_proj"][i]),
    )


def _state_to_blocks(s):
    b = s.shape[0]
    eye = jnp.eye(GLA_HEADS, dtype=s.dtype)
    blocks = jnp.einsum("bhde,hg->bhegd", s, eye)
    return blocks.reshape(b, GLA_VAL_WIDTH, GLA_KEY_WIDTH)


def _blocks_to_state(st):
    b = st.shape[0]
    blocks = st.reshape(b, GLA_HEADS, GLA_VAL_DIM, GLA_HEADS, GLA_KEY_DIM)
    diag = jnp.stack([blocks[:, h, :, h, :] for h in range(GLA_HEADS)], axis=1)
    return diag.transpose(0, 1, 3, 2)


def _run_group(x, p, k_cache, v_cache, s_gla, s_pool, layers, final_norm):
    b, t, _ = x.shape
    n = b * t
    depth = len(layers)
    has_cache = k_cache is not None
    n_past = k_cache.shape[2] if has_cache else 0
    x = x.reshape(n, D_MODEL)
    ks, vs, ss, ps = [], [], [], []
    for i, w in enumerate(layers):
        x = _ffn(x, w)
        q, k, v = _qkv(x, w)
        q, k3, v3 = (a.reshape(b, t, SB_WIDTH) for a in (q, k, v))
        if has_cache:
            kc = k_cache[i].reshape(b, n_past, SB_WIDTH)
            vc = v_cache[i].reshape(b, n_past, SB_WIDTH)
            st0 = _state_to_blocks(s_gla[i])
            hist0 = jnp.pad(s_pool[i], ((0, 0), (POOL_HIST_ROWS - POOL_HIST, 0), (0, 0)))
        else:
            kc = vc = None
            st0 = jnp.zeros((b, GLA_VAL_WIDTH, GLA_KEY_WIDTH), F32)
            hist0 = jnp.zeros((b, POOL_HIST_ROWS, POOL_WIDTH), F32)
        y_a = _sb_attention(q, k3, v3, kc, vc)
        x3, st, hist = _mixer(x.reshape(b, t, D_MODEL), y_a, st0, hist0, w, n_past)
        x = _ffn_ple(x3.reshape(n, D_MODEL), p[i].reshape(n, PLE_DIM), w, final_norm, final=(i == depth - 1))
        ks.append(k.reshape(b, t, SB_HEADS, SB_HEAD_DIM))
        vs.append(v.reshape(b, t, SB_HEADS, SB_HEAD_DIM))
        ss.append(_blocks_to_state(st))
        ps.append(hist[:, POOL_HIST_ROWS - POOL_HIST:, :])
    return x.reshape(b, t, D_MODEL), jnp.stack(ks), jnp.stack(vs), jnp.stack(ss), jnp.stack(ps)


def kernel(x_prompt, x_sample, cache_sb_k, cache_sb_v, state_gla, state_pool, p_prompt, p_sample, ffn1_norm, ffn1_w_in, ffn1_w_out, mix_norm, w_in, gla_w_gate, gla_b_gate, gla_norm, pool_w, pool_scale, w_branch_a, w_branch_b, w_branch_c, w_out, ffn2_norm, ffn2_w_in, ffn2_w_out, ple_norm, ple_w_gate, ple_w_proj, final_norm):
    params = dict(ffn1_norm=ffn1_norm, ffn1_w_in=ffn1_w_in, ffn1_w_out=ffn1_w_out, mix_norm=mix_norm, w_in=w_in,
                  gla_w_gate=gla_w_gate, gla_b_gate=gla_b_gate, gla_norm=gla_norm, pool_w=pool_w, pool_scale=pool_scale,
                  w_branch_a=w_branch_a, w_branch_b=w_branch_b, w_branch_c=w_branch_c, w_out=w_out,
                  ffn2_norm=ffn2_norm, ffn2_w_in=ffn2_w_in, ffn2_w_out=ffn2_w_out, ple_norm=ple_norm,
                  ple_w_gate=ple_w_gate, ple_w_proj=ple_w_proj)
    depth = w_in.shape[0]
    layers = [_layer_weights(i, params) for i in range(depth)]
    fnorm = final_norm.reshape(1, D_MODEL).astype(F32)
    prompt = _run_group(x_prompt, p_prompt, None, None, None, None, layers, fnorm)
    sample = _run_group(x_sample, p_sample, cache_sb_k, cache_sb_v, state_gla, state_pool, layers, fnorm)
    return (prompt[0], sample[0], prompt[1], prompt[2], prompt[3], prompt[4],
            sample[1], sample[2], sample[3], sample[4])
```

```python
import functools

import jax
import jax.numpy as jnp
from jax import lax
from jax.experimental import pallas as pl
from jax.experimental.pallas import tpu as pltpu

D_MODEL = 1024
FFN_DIM = 2816
PLE_DIM = 256
SB_HEADS = 8
SB_HEAD_DIM = 64
SB_WIDTH = SB_HEADS * SB_HEAD_DIM
GLA_HEADS = 4
GLA_KEY_DIM = 32
GLA_VAL_DIM = 64
GLA_KEY_WIDTH = GLA_HEADS * GLA_KEY_DIM
GLA_VAL_WIDTH = GLA_HEADS * GLA_VAL_DIM
GLA_GATE_RANK = 16
GLA_GATE_TAU = 16.0
GLA_CHUNK = 64
POOL_WINDOWS = (2, 4, 8, 16)
POOL_GROUP_DIM = 64
POOL_WIDTH = len(POOL_WINDOWS) * POOL_GROUP_DIM
POOL_HIST = 15
POOL_HIST_ROWS = 16
RMS_EPS = 1e-6

LANES = 128
VMEM_LIMIT_BYTES = 56 * 1024 * 1024
TOKEN_TILE = 512
FFN_CHUNK = 256
SB_QUERY_BLOCK = 128
SB_KEY_BLOCK = 128
SB_LOG_ZERO = -104.0

F32 = jnp.float32
BF16 = jnp.bfloat16


def _dot(a, b):
    return jnp.dot(a, b, preferred_element_type=F32)


def _dot_nt(a, b):
    return lax.dot_general(a, b, (((1,), (1,)), ((), ())), preferred_element_type=F32)


def _dot_tn(a, b):
    return lax.dot_general(a, b, (((0,), (0,)), ((), ())), preferred_element_type=F32)


def _dot_split(a, b):
    hi = a.astype(BF16)
    lo = (a - hi.astype(F32)).astype(BF16)
    return _dot(hi, b) + _dot(lo, b)


def _iota_div(shape, axis, size):
    assert size & (size - 1) == 0
    return lax.broadcasted_iota(jnp.int32, shape, axis) >> (size.bit_length() - 1)


def _rms(x, g):
    ms = jnp.mean(x * x, axis=-1, keepdims=True)
    return x * lax.rsqrt(ms + RMS_EPS) * g


def _params(*semantics):
    return pltpu.CompilerParams(dimension_semantics=semantics, vmem_limit_bytes=VMEM_LIMIT_BYTES)


def _const_spec(shape):
    zeros = (0,) * len(shape)
    return pl.BlockSpec(shape, lambda *_: zeros, pipeline_mode=pl.Buffered(1))


def _swiglu_update(x, norm_ref, wa_ref, wb_ref, wo_ref):
    h = _rms(x, norm_ref[...]).astype(BF16)
    acc = jnp.zeros(x.shape, F32)
    for f in range(0, FFN_DIM, FFN_CHUNK):
        a = _dot(h, wa_ref[:, f:f + FFN_CHUNK])
        b = _dot(h, wb_ref[:, f:f + FFN_CHUNK])
        g = (a * jax.nn.sigmoid(a) * b).astype(BF16)
        acc = acc + _dot(g, wo_ref[f:f + FFN_CHUNK, :])
    return x + 0.5 * acc


def _ffn_kernel(x_ref, norm_ref, wa_ref, wb_ref, wo_ref, o_ref):
    o_ref[...] = _swiglu_update(x_ref[...], norm_ref, wa_ref, wb_ref, wo_ref)


def _ffn_ple_kernel(x_ref, p_ref, norm_ref, wa_ref, wb_ref, wo_ref, pnorm_ref, wpg_ref, wpp_ref, fnorm_ref,
                    o_ref, *, final):
    x = _swiglu_update(x_ref[...], norm_ref, wa_ref, wb_ref, wo_ref)
    gate = jax.nn.sigmoid(_dot(_rms(x, pnorm_ref[...]).astype(BF16), wpg_ref[...]))
    x = x + gate * _dot(p_ref[...].astype(BF16), wpp_ref[...])
    if final:
        x = _rms(x, fnorm_ref[...])
    o_ref[...] = x


def _ffn(x, w):
    n = x.shape[0]
    tok = lambda i: (i, 0)
    return pl.pallas_call(
        _ffn_kernel,
        grid=(n // TOKEN_TILE,),
        in_specs=[pl.BlockSpec((TOKEN_TILE, D_MODEL), tok),
                  _const_spec((1, D_MODEL)),
                  _const_spec((D_MODEL, FFN_DIM)),
                  _const_spec((D_MODEL, FFN_DIM)),
                  _const_spec((FFN_DIM, D_MODEL))],
        out_specs=pl.BlockSpec((TOKEN_TILE, D_MODEL), tok),
        out_shape=jax.ShapeDtypeStruct((n, D_MODEL), F32),
        compiler_params=_params("parallel"),
        name="ffn",
    )(x, w["ffn1_norm"], w["ffn1_wa"], w["ffn1_wb"], w["ffn1_wo"])


def _ffn_ple(x, p, w, final_norm, final):
    n = x.shape[0]
    tok = lambda i: (i, 0)
    return pl.pallas_call(
        functools.partial(_ffn_ple_kernel, final=final),
        grid=(n // TOKEN_TILE,),
        in_specs=[pl.BlockSpec((TOKEN_TILE, D_MODEL), tok),
                  pl.BlockSpec((TOKEN_TILE, PLE_DIM), tok),
                  _const_spec((1, D_MODEL)),
                  _const_spec((D_MODEL, FFN_DIM)),
                  _const_spec((D_MODEL, FFN_DIM)),
                  _const_spec((FFN_DIM, D_MODEL)),
                  _const_spec((1, D_MODEL)),
                  _const_spec((D_MODEL, D_MODEL)),
                  _const_spec((PLE_DIM, D_MODEL)),
                  _const_spec((1, D_MODEL))],
        out_specs=pl.BlockSpec((TOKEN_TILE, D_MODEL), tok),
        out_shape=jax.ShapeDtypeStruct((n, D_MODEL), F32),
        compiler_params=_params("parallel"),
        name="ffn_ple",
    )(x, p, w["ffn2_norm"], w["ffn2_wa"], w["ffn2_wb"], w["ffn2_wo"], w["ple_norm"], w["ple_wg"], w["ple_wp"],
      final_norm)


def _qkv_kernel(x_ref, norm_ref, w_ref, k_all_ref, v_all_ref, q_ref, kb_ref, vb_ref, k_ref, v_ref):
    del k_all_ref, v_all_ref
    h = _rms(x_ref[...], norm_ref[...]).astype(BF16)
    u = _dot(h, w_ref[...])
    q_ref[...] = (u[:, :SB_WIDTH] * SB_HEAD_DIM ** -0.5).astype(BF16)
    k = u[:, SB_WIDTH:2 * SB_WIDTH]
    v = u[:, 2 * SB_WIDTH:]
    k_ref[...] = k
    v_ref[...] = v
    kb_ref[...] = k.astype(BF16)
    vb_ref[...] = v.astype(BF16)


def _qkv(x, w, k_all, v_all, layer):
    n = x.shape[0]
    tok = lambda i: (i, 0)
    layer_tok = lambda i: (layer, i, 0)
    return pl.pallas_call(
        _qkv_kernel,
        grid=(n // TOKEN_TILE,),
        in_specs=[pl.BlockSpec((TOKEN_TILE, D_MODEL), tok),
                  _const_spec((1, D_MODEL)),
                  _const_spec((D_MODEL, 3 * SB_WIDTH)),
                  pl.BlockSpec(memory_space=pl.ANY),
                  pl.BlockSpec(memory_space=pl.ANY)],
        out_specs=[pl.BlockSpec((TOKEN_TILE, SB_WIDTH), tok)] * 3
        + [pl.BlockSpec((None, TOKEN_TILE, SB_WIDTH), layer_tok)] * 2,
        out_shape=[jax.ShapeDtypeStruct((n, SB_WIDTH), BF16)] * 3
        + [jax.ShapeDtypeStruct(k_all.shape, F32), jax.ShapeDtypeStruct(v_all.shape, F32)],
        input_output_aliases={3: 3, 4: 4},
        compiler_params=_params("parallel"),
        name="qkv",
    )(x, w["mix_norm"], w["w_qkv"], k_all, v_all)


def _sb_blocks(q2, blocks, c, acc, tri_ones, head_lanes):
    pairs = len(q2)
    qb = acc[0].shape[0]
    n = tri_ones.shape[0] // 2
    log_beta, split = [], []
    for k, _, mask in blocks:
        for p in range(pairs):
            z = _dot_nt(q2[p], k[p])
            t = jnp.log(1.0 + jnp.exp(-jnp.abs(z)))
            lk = -(jnp.maximum(z, 0.0) + t)
            if mask is not None:
                lk = jnp.where(mask, lk, 0.0)
            hi = lk.astype(BF16)
            lo = (lk - hi.astype(F32)).astype(BF16)
            log_beta.append(jnp.minimum(z, 0.0) - t)
            split.append(jnp.concatenate([hi, lo], axis=1))
    x = _dot(jnp.concatenate(split, axis=0), tri_ones)
    c, acc = list(c), list(acc)
    for b, (_, v, mask) in enumerate(blocks):
        for p in range(pairs):
            r0 = (b * pairs + p) * 2 * qb
            xs = x[r0:r0 + 2 * qb]
            w = jnp.exp(log_beta[b * pairs + p] + xs[:, :n] + c[p])
            if mask is not None:
                w = jnp.where(mask, w, 0.0)
            c[p] = c[p] + xs[:, n:]
            w2 = jnp.concatenate([w[:qb], w[qb:]], axis=1).astype(BF16)
            v2 = jnp.concatenate([jnp.where(m, v[p], jnp.zeros_like(v[p])) for m in head_lanes], axis=0)
            acc[p] = acc[p] + _dot(w2, v2)
    return c, acc


def _sb_kernel(*refs, qb, n_cache):
    if n_cache:
        q_ref, kn_ref, vn_ref, kc_ref, vc_ref, o_ref, acc_s, c_s = refs
    else:
        q_ref, kn_ref, vn_ref, o_ref, acc_s, c_s = refs
        kc_ref, vc_ref = kn_ref, vn_ref
    i = pl.program_id(1)
    n = SB_KEY_BLOCK
    pairs = SB_WIDTH // LANES
    lead = n - qb
    assert lead == 0 or (n_cache >= lead + 2 * n and kn_ref.shape[1] == qb)

    lane = lax.broadcasted_iota(jnp.int32, (1, LANES), 1)
    head_lanes = (lane < SB_HEAD_DIM, lane >= SB_HEAD_DIM)
    tri_row = lax.broadcasted_iota(jnp.int32, (2 * n, 2 * n), 0) & (n - 1)
    tri_col = lax.broadcasted_iota(jnp.int32, (2 * n, 2 * n), 1)
    tri_ones = jnp.logical_or(tri_col >= n, tri_row > tri_col).astype(BF16)
    key_s = lax.broadcasted_iota(jnp.int32, (1, n), 1)
    row_t = lax.broadcasted_iota(jnp.int32, (2 * qb, 1), 0) & (qb - 1)
    causal = key_s < row_t + lead

    def lanes(p):
        return slice(p * LANES, (p + 1) * LANES)

    def stacked_q():
        q2 = []
        for p in range(pairs):
            qp = q_ref[0, :, lanes(p)]
            q2.append(jnp.concatenate([jnp.where(m, qp, jnp.zeros_like(qp)) for m in head_lanes], axis=0))
        return q2

    def earlier(start, mask):
        rows = pl.ds(start, n)
        return ([kc_ref[0, rows, lanes(p)].astype(BF16) for p in range(pairs)],
                [vc_ref[0, rows, lanes(p)].astype(BF16) for p in range(pairs)], mask)

    def first_block():
        own = pl.ds(pl.multiple_of(i * qb, qb), qb)
        old = slice(n_cache - lead, n_cache)
        ks, vs = [], []
        for p in range(pairs):
            for new_ref, old_ref, out in ((kn_ref, kc_ref, ks), (vn_ref, vc_ref, vs)):
                blk = new_ref[0, own, lanes(p)].astype(BF16)
                if lead:
                    blk = jnp.concatenate([old_ref[0, old, lanes(p)].astype(BF16), blk], axis=0)
                out.append(blk)
        return ks, vs, causal

    def run(blocks, fresh_start):
        if fresh_start:
            c = [jnp.zeros((2 * qb, n), F32)] * pairs
            acc = [jnp.zeros((qb, LANES), F32)] * pairs
        else:
            c = [c_s[p] for p in range(pairs)]
            acc = [acc_s[p] for p in range(pairs)]
        c, acc = _sb_blocks(stacked_q(), blocks, c, acc, tri_ones, head_lanes)
        for p in range(pairs):
            c_s[p], acc_s[p] = c[p], acc[p]

    unvisited = n_cache + i * qb - lead

    def first_three():
        starts = [unvisited - d * n for d in (1, 2)]
        if not n_cache:
            starts = [pl.multiple_of(s, n) for s in starts]
        run([first_block()] + [earlier(s, None) for s in starts], True)

    if n_cache:
        first_three()
        unvisited = jnp.int32(unvisited - 2 * n)
    else:
        pl.when(i >= 2)(first_three)
        pl.when(i < 2)(lambda: run([first_block()], True))
        unvisited = jnp.where(i >= 2, unvisited - 2 * n, unvisited)

    def live(rem):
        return jnp.logical_and(rem > 0, jnp.max(c_s[...]) >= SB_LOG_ZERO)

    def one_block(rem):
        start = pl.multiple_of(jnp.maximum(rem - n, 0), qb)
        fresh = start + key_s < rem
        run([earlier(start, fresh)], False)
        return start

    lax.while_loop(live, one_block, unvisited)
    o_ref[0] = jnp.concatenate([acc_s[p] for p in range(pairs)], axis=1).astype(o_ref.dtype)


def _sb_attention(q, k_new, v_new, k_cache, v_cache):
    b, t, _ = q.shape
    n_cache = 0 if k_cache is None else k_cache.shape[1]
    qb = min(SB_QUERY_BLOCK, t)
    assert t % qb == 0 and (n_cache == 0 or t == qb)
    pairs = SB_WIDTH // LANES
    blk = lambda bi, i: (bi, i, 0)
    full = lambda bi, i: (bi, 0, 0)
    in_specs = [pl.BlockSpec((1, qb, SB_WIDTH), blk),
                pl.BlockSpec((1, t, SB_WIDTH), full),
                pl.BlockSpec((1, t, SB_WIDTH), full)]
    args = [q, k_new, v_new]
    if n_cache:
        in_specs += [pl.BlockSpec((1, n_cache, SB_WIDTH), full)] * 2
        args += [k_cache, v_cache]
    return pl.pallas_call(
        functools.partial(_sb_kernel, qb=qb, n_cache=n_cache),
        grid=(b, t // qb),
        in_specs=in_specs,
        out_specs=pl.BlockSpec((1, qb, SB_WIDTH), blk),
        out_shape=jax.ShapeDtypeStruct((b, t, SB_WIDTH), BF16),
        scratch_shapes=[pltpu.VMEM((pairs, qb, LANES), F32), pltpu.VMEM((pairs, 2 * qb, SB_KEY_BLOCK), F32)],
        compiler_params=_params("parallel", "arbitrary"),
        name="sb_attention",
    )(*args)


def _gla_chunk(q, k, v, g, st, masks):
    tril, causal, k_heads, v_heads, st_heads = masks
    g_hi = g.astype(BF16)
    g_lo = (g - g_hi.astype(F32)).astype(BF16)
    b = _dot(tril, g_hi) + _dot(tril, g_lo)
    mid = b[GLA_CHUNK // 2 - 1:GLA_CHUNK // 2, :]
    last = b[GLA_CHUNK - 1:GLA_CHUNK, :]
    q_in = (q * jnp.exp(b - mid)).astype(BF16)
    k_in = k * jnp.exp(mid - b)
    k_rows = (jnp.concatenate([k_in] * GLA_HEADS, axis=0) * k_heads).astype(BF16)
    v_bf = v.astype(BF16)
    v_rows = jnp.concatenate([v_bf] * GLA_HEADS, axis=0) * v_heads
    scores = jnp.where(causal, _dot_nt(q_in, k_rows), 0.0).astype(BF16)
    intra = _dot(scores, v_rows)
    inter = _dot_nt((q * jnp.exp(b)).astype(BF16), st.astype(BF16))
    k_out = (k * jnp.exp(last - b)).astype(BF16)
    st_new = st * jnp.exp(last) + _dot_tn(v_bf, k_out) * st_heads
    return inter + intra, st_new


def _mixer_kernel(x_ref, ya_ref, st0_ref, hist0_ref, norm_ref, wp_ref, wgate_ref, glawg_ref, glabg_ref, glanorm_ref,
                  poolw_ref, poolscale_ref, wa_ref, wb_ref, wc_ref, wout_ref,
                  o_ref, st_ref, hist_ref, st_s, ext_s, *, seqs, rows, n_past):
    t = pl.program_id(1)
    m = seqs * rows

    @pl.when(t == 0)
    def _():
        st_s[...] = st0_ref[...]
        ext_s[:, 0:POOL_HIST_ROWS, :] = hist0_ref[...]

    x = x_ref[...].reshape(m, D_MODEL)
    h = _rms(x, norm_ref[...]).astype(BF16)
    u = _dot(h, wp_ref[...])
    kw, vw = GLA_KEY_WIDTH, GLA_VAL_WIDTH
    q_b = u[:, 0:kw] * GLA_KEY_DIM ** -0.5
    k_b = u[:, kw:2 * kw]
    v_b = u[:, 2 * kw:2 * kw + vw]
    r_b = u[:, 2 * kw + vw:3 * kw + vw]
    o_b = u[:, 3 * kw + vw:3 * kw + 2 * vw]
    u_c = u[:, 3 * kw + 2 * vw:]

    gate = _dot(r_b.astype(BF16), glawg_ref[...]) + glabg_ref[...]
    log_alpha = (jnp.minimum(gate, 0.0) - jnp.log1p(jnp.exp(-jnp.abs(gate)))) / GLA_GATE_TAU
    c = GLA_CHUNK
    ri = lax.broadcasted_iota(jnp.int32, (c, c), 0)
    ci = lax.broadcasted_iota(jnp.int32, (c, c), 1)
    tril = (ci <= ri).astype(BF16)
    key_head = _iota_div((1, kw), 1, GLA_KEY_DIM)
    val_head = _iota_div((1, vw), 1, GLA_VAL_DIM)
    rows_hs = _iota_div((GLA_HEADS * c, 1), 0, c)
    causal = (lax.broadcasted_iota(jnp.int32, (c, GLA_HEADS * c), 1) & (c - 1)) <= lax.broadcasted_iota(
        jnp.int32, (c, GLA_HEADS * c), 0)
    k_heads = (rows_hs == key_head).astype(F32)
    v_heads = (rows_hs == val_head).astype(BF16)
    st_heads = (_iota_div((vw, 1), 0, GLA_VAL_DIM) == key_head).astype(F32)
    masks = (tril, causal, k_heads, v_heads, st_heads)
    o_parts = []
    for s in range(seqs):
        st = st_s[s]
        for j in range(rows // c):
            r0 = s * rows + j * c
            o, st = _gla_chunk(q_b[r0:r0 + c], k_b[r0:r0 + c], v_b[r0:r0 + c], log_alpha[r0:r0 + c], st, masks)
            o_parts.append(o)
        st_s[s] = st
    o = jnp.concatenate(o_parts, axis=0)
    head_mean = (_iota_div((vw, 1), 0, GLA_VAL_DIM) == val_head).astype(BF16) * (1.0 / GLA_VAL_DIM)
    ms = _dot_split(o * o, head_mean)
    y_b = o * lax.rsqrt(ms + RMS_EPS) * glanorm_ref[...] * (o_b * jax.nn.sigmoid(o_b))

    lane = lax.broadcasted_iota(jnp.int32, (1, POOL_WIDTH), 1)
    pos1 = n_past + t * rows + 1 + lax.broadcasted_iota(jnp.int32, (rows, 1), 0)
    d_parts = []
    for s in range(seqs):
        ext_s[s, POOL_HIST_ROWS:, :] = u_c[s * rows:(s + 1) * rows]
        e = ext_s[s]
        sums = []
        span = 1
        for w in POOL_WINDOWS:
            while span < w:
                e = e + pltpu.roll(e, span, 0)
                span *= 2
            sums.append(e[POOL_HIST_ROWS:])
        wsum = sums[-1]
        cnt = jnp.minimum(POOL_WINDOWS[-1], pos1)
        for gi in range(len(POOL_WINDOWS) - 2, -1, -1):
            in_group = lane < (gi + 1) * POOL_GROUP_DIM
            wsum = jnp.where(in_group, sums[gi], wsum)
            cnt = jnp.where(in_group, jnp.minimum(POOL_WINDOWS[gi], pos1), cnt)
        d_parts.append(wsum / cnt.astype(F32) - u_c[s * rows:(s + 1) * rows])
        ext_s[s, 0:POOL_HIST_ROWS, :] = ext_s[s, rows:rows + POOL_HIST_ROWS, :]
    d = jnp.concatenate(d_parts, axis=0)
    y_c = _dot(d.astype(BF16), poolw_ref[...]) * poolscale_ref[...]

    branches = ((ya_ref[...].reshape(m, SB_WIDTH), wa_ref), (y_b.astype(BF16), wb_ref), (y_c.astype(BF16), wc_ref))
    merged = None
    for bi, (y, w_ref) in enumerate(branches):
        gate_b = jax.nn.sigmoid(_dot(h, wgate_ref[:, bi * D_MODEL:(bi + 1) * D_MODEL]))
        term = gate_b * _dot(y, w_ref[...])
        merged = term if merged is None else merged + term
    o_ref[...] = (x + _dot(merged.astype(BF16), wout_ref[...])).reshape(seqs, rows, D_MODEL)

    @pl.when(t == pl.num_programs(1) - 1)
    def _():
        st_ref[...] = st_s[...]
        hist_ref[...] = ext_s[:, 0:POOL_HIST_ROWS, :]


def _mixer(x, y_a, st0, hist0, w, n_past):
    b, t, _ = x.shape
    rows = min(TOKEN_TILE, t)
    seqs = TOKEN_TILE // rows
    assert t % rows == 0 and b % seqs == 0 and rows % GLA_CHUNK == 0
    tok = lambda bi, ti: (bi, ti, 0)
    per_seq = lambda bi, ti: (bi, 0, 0)
    kw, vw = GLA_KEY_WIDTH, GLA_VAL_WIDTH
    return pl.pallas_call(
        functools.partial(_mixer_kernel, seqs=seqs, rows=rows, n_past=n_past),
        grid=(b // seqs, t // rows),
        in_specs=[pl.BlockSpec((seqs, rows, D_MODEL), tok),
                  pl.BlockSpec((seqs, rows, SB_WIDTH), tok),
                  pl.BlockSpec((seqs, vw, kw), per_seq),
                  pl.BlockSpec((seqs, POOL_HIST_ROWS, POOL_WIDTH), per_seq),
                  _const_spec((1, D_MODEL)),
                  _const_spec((D_MODEL, 3 * kw + 3 * vw)),
                  _const_spec((D_MODEL, 3 * D_MODEL)),
                  _const_spec((kw, kw)),
                  _const_spec((1, kw)),
                  _const_spec((1, vw)),
                  _const_spec((POOL_WIDTH, POOL_WIDTH)),
                  _const_spec((1, POOL_WIDTH)),
                  _const_spec((SB_WIDTH, D_MODEL)),
                  _const_spec((vw, D_MODEL)),
                  _const_spec((POOL_WIDTH, D_MODEL)),
                  _const_spec((D_MODEL, D_MODEL))],
        out_specs=[pl.BlockSpec((seqs, rows, D_MODEL), tok),
                   pl.BlockSpec((seqs, vw, kw), per_seq),
                   pl.BlockSpec((seqs, POOL_HIST_ROWS, POOL_WIDTH), per_seq)],
        out_shape=[jax.ShapeDtypeStruct((b, t, D_MODEL), F32),
                   jax.ShapeDtypeStruct((b, vw, kw), F32),
                   jax.ShapeDtypeStruct((b, POOL_HIST_ROWS, POOL_WIDTH), F32)],
        scratch_shapes=[pltpu.VMEM((seqs, vw, kw), F32),
                        pltpu.VMEM((seqs, POOL_HIST_ROWS + rows, POOL_WIDTH), F32)],
        compiler_params=_params("parallel", "arbitrary"),
        name="mixer",
    )(x, y_a, st0, hist0, w["mix_norm"], w["w_proj"], w["w_gate"], w["gla_wg"], w["gla_bg"], w["gla_norm"],
      w["pool_w"], w["pool_scale"], w["w_branch_a"], w["w_branch_b"], w["w_branch_c"], w["w_out"])


def _layer_weights(i, p):
    row = lambda a: a[i].reshape(1, -1).astype(F32)
    bf = lambda a: a.astype(BF16)
    kw, vw = GLA_KEY_WIDTH, GLA_VAL_WIDTH
    w_in = p["w_in"][i]
    a_end = 3 * SB_WIDTH
    qkv_b = w_in[:, a_end:a_end + 2 * kw + vw]
    r0 = a_end + 2 * kw + vw
    r_b = jnp.pad(w_in[:, r0:r0 + GLA_GATE_RANK], ((0, 0), (0, kw - GLA_GATE_RANK)))
    o0 = r0 + GLA_GATE_RANK
    rest = w_in[:, o0:o0 + vw + POOL_WIDTH]
    g0 = o0 + vw + POOL_WIDTH
    pool_w = jnp.zeros((POOL_WIDTH, POOL_WIDTH), F32)
    for gi in range(len(POOL_WINDOWS)):
        lo = gi * POOL_GROUP_DIM
        pool_w = pool_w.at[lo:lo + POOL_GROUP_DIM, lo:lo + POOL_GROUP_DIM].set(p["pool_w"][i, gi])
    return dict(
        ffn1_norm=row(p["ffn1_norm"]), ffn1_wa=bf(p["ffn1_w_in"][i, :, :FFN_DIM]), ffn1_wb=bf(p["ffn1_w_in"][i, :, FFN_DIM:]),
        ffn1_wo=bf(p["ffn1_w_out"][i]),
        mix_norm=row(p["mix_norm"]), w_qkv=bf(w_in[:, :a_end]),
        w_proj=bf(jnp.concatenate([qkv_b, r_b, rest], axis=1)), w_gate=bf(w_in[:, g0:]),
        gla_wg=bf(jnp.pad(p["gla_w_gate"][i], ((0, kw - GLA_GATE_RANK), (0, 0)))),
        gla_bg=row(p["gla_b_gate"]), gla_norm=row(p["gla_norm"]),
        pool_w=bf(pool_w), pool_scale=row(p["pool_scale"]),
        w_branch_a=bf(p["w_branch_a"][i]), w_branch_b=bf(p["w_branch_b"][i]), w_branch_c=bf(p["w_branch_c"][i]),
        w_out=bf(p["w_out"][i]),
        ffn2_norm=row(p["ffn2_norm"]), ffn2_wa=bf(p["ffn2_w_in"][i, :, :FFN_DIM]), ffn2_wb=bf(p["ffn2_w_in"][i, :, FFN_DIM:]),
        ffn2_wo=bf(p["ffn2_w_out"][i]),
        ple_norm=row(p["ple_norm"]), ple_wg=bf(p["ple_w_gate"][i]), ple_wp=bf(p["ple_w_proj"][i]),
    )


def _state_to_blocks(s):
    b = s.shape[0]
    eye = jnp.eye(GLA_HEADS, dtype=s.dtype)
    blocks = jnp.einsum("bhde,hg->bhegd", s, eye)
    return blocks.reshape(b, GLA_VAL_WIDTH, GLA_KEY_WIDTH)


def _blocks_to_state(st):
    b = st.shape[0]
    blocks = st.reshape(b, GLA_HEADS, GLA_VAL_DIM, GLA_HEADS, GLA_KEY_DIM)
    diag = jnp.stack([blocks[:, h, :, h, :] for h in range(GLA_HEADS)], axis=1)
    return diag.transpose(0, 1, 3, 2)


def _run_group(x, p, k_cache, v_cache, s_gla, s_pool, layers, final_norm):
    b, t, _ = x.shape
    n = b * t
    depth = len(layers)
    has_cache = k_cache is not None
    n_past = k_cache.shape[2] if has_cache else 0
    x = x.reshape(n, D_MODEL)
    k_all = jnp.zeros((depth, n, SB_WIDTH), F32)
    v_all = jnp.zeros((depth, n, SB_WIDTH), F32)
    ss, ps = [], []
    for i, w in enumerate(layers):
        x = _ffn(x, w)
        q, k, v, k_all, v_all = _qkv(x, w, k_all, v_all, i)
        q, k3, v3 = (a.reshape(b, t, SB_WIDTH) for a in (q, k, v))
        if has_cache:
            kc = k_cache[i].reshape(b, n_past, SB_WIDTH)
            vc = v_cache[i].reshape(b, n_past, SB_WIDTH)
            st0 = _state_to_blocks(s_gla[i])
            hist0 = jnp.pad(s_pool[i], ((0, 0), (POOL_HIST_ROWS - POOL_HIST, 0), (0, 0)))
        else:
            kc = vc = None
            st0 = jnp.zeros((b, GLA_VAL_WIDTH, GLA_KEY_WIDTH), F32)
            hist0 = jnp.zeros((b, POOL_HIST_ROWS, POOL_WIDTH), F32)
        y_a = _sb_attention(q, k3, v3, kc, vc)
        x3, st, hist = _mixer(x.reshape(b, t, D_MODEL), y_a, st0, hist0, w, n_past)
        x = _ffn_ple(x3.reshape(n, D_MODEL), p[i].reshape(n, PLE_DIM), w, final_norm, final=(i == depth - 1))
        ss.append(_blocks_to_state(st))
        ps.append(hist[:, POOL_HIST_ROWS - POOL_HIST:, :])
    heads = (depth, b, t, SB_HEADS, SB_HEAD_DIM)
    return x.reshape(b, t, D_MODEL), k_all.reshape(heads), v_all.reshape(heads), jnp.stack(ss), jnp.stack(ps)


def kernel(x_prompt, x_sample, cache_sb_k, cache_sb_v, state_gla, state_pool, p_prompt, p_sample, ffn1_norm, ffn1_w_in, ffn1_w_out, mix_norm, w_in, gla_w_gate, gla_b_gate, gla_norm, pool_w, pool_scale, w_branch_a, w_branch_b, w_branch_c, w_out, ffn2_norm, ffn2_w_in, ffn2_w_out, ple_norm, ple_w_gate, ple_w_proj, final_norm):
    params = dict(ffn1_norm=ffn1_norm, ffn1_w_in=ffn1_w_in, ffn1_w_out=ffn1_w_out, mix_norm=mix_norm, w_in=w_in,
                  gla_w_gate=gla_w_gate, gla_b_gate=gla_b_gate, gla_norm=gla_norm, pool_w=pool_w, pool_scale=pool_scale,
                  w_branch_a=w_branch_a, w_branch_b=w_branch_b, w_branch_c=w_branch_c, w_out=w_out,
                  ffn2_norm=ffn2_norm, ffn2_w_in=ffn2_w_in, ffn2_w_out=ffn2_w_out, ple_norm=ple_norm,
                  ple_w_gate=ple_w_gate, ple_w_proj=ple_w_proj)
    depth = w_in.shape[0]
    layers = [_layer_weights(i, params) for i in range(depth)]
    fnorm = final_norm.reshape(1, D_MODEL).astype(F32)
    prompt = _run_group(x_prompt, p_prompt, None, None, None, None, layers, fnorm)
    sample = _run_group(x_sample, p_sample, cache_sb_k, cache_sb_v, state_gla, state_pool, layers, fnorm)
    return (prompt[0], sample[0], prompt[1], prompt[2], prompt[3], prompt[4],
            sample[1], sample[2], sample[3], sample[4])
```

```python
import functools

import jax
import jax.numpy as jnp
from jax import lax
from jax.experimental import pallas as pl
from jax.experimental.pallas import tpu as pltpu

D_MODEL = 1024
FFN_DIM = 2816
PLE_DIM = 256
SB_HEADS = 8
SB_HEAD_DIM = 64
SB_WIDTH = SB_HEADS * SB_HEAD_DIM
GLA_HEADS = 4
GLA_KEY_DIM = 32
GLA_VAL_DIM = 64
GLA_KEY_WIDTH = GLA_HEADS * GLA_KEY_DIM
GLA_VAL_WIDTH = GLA_HEADS * GLA_VAL_DIM
GLA_GATE_RANK = 16
GLA_GATE_TAU = 16.0
GLA_CHUNK = 64
POOL_WINDOWS = (2, 4, 8, 16)
POOL_GROUP_DIM = 64
POOL_WIDTH = len(POOL_WINDOWS) * POOL_GROUP_DIM
POOL_HIST = 15
POOL_HIST_ROWS = 16
RMS_EPS = 1e-6

LANES = 128
VMEM_LIMIT_BYTES = 56 * 1024 * 1024
TOKEN_TILE = 512
FFN_CHUNK = 256
SB_QUERY_BLOCK = 128
SB_KEY_BLOCK = 128
SB_LOG_ZERO = -104.0

F32 = jnp.float32
BF16 = jnp.bfloat16


def _dot(a, b):
    return jnp.dot(a, b, preferred_element_type=F32)


def _dot_nt(a, b):
    return lax.dot_general(a, b, (((1,), (1,)), ((), ())), preferred_element_type=F32)


def _dot_tn(a, b):
    return lax.dot_general(a, b, (((0,), (0,)), ((), ())), preferred_element_type=F32)


def _dot_split(a, b):
    hi = a.astype(BF16)
    lo = (a - hi.astype(F32)).astype(BF16)
    return _dot(hi, b) + _dot(lo, b)


def _iota_div(shape, axis, size):
    assert size & (size - 1) == 0
    return lax.broadcasted_iota(jnp.int32, shape, axis) >> (size.bit_length() - 1)


def _rms(x, g):
    ms = jnp.mean(x * x, axis=-1, keepdims=True)
    return x * lax.rsqrt(ms + RMS_EPS) * g


def _params(*semantics):
    return pltpu.CompilerParams(dimension_semantics=semantics, vmem_limit_bytes=VMEM_LIMIT_BYTES)


def _const_spec(shape):
    zeros = (0,) * len(shape)
    return pl.BlockSpec(shape, lambda *_: zeros, pipeline_mode=pl.Buffered(1))


def _swiglu_update(x, norm_ref, wa_ref, wb_ref, wo_ref):
    h = _rms(x, norm_ref[...]).astype(BF16)
    acc = jnp.zeros(x.shape, F32)
    for f in range(0, FFN_DIM, FFN_CHUNK):
        a = _dot(h, wa_ref[:, f:f + FFN_CHUNK])
        b = _dot(h, wb_ref[:, f:f + FFN_CHUNK])
        g = (a * jax.nn.sigmoid(a) * b).astype(BF16)
        acc = acc + _dot(g, wo_ref[f:f + FFN_CHUNK, :])
    return x + 0.5 * acc


def _ffn_kernel(x_ref, norm_ref, wa_ref, wb_ref, wo_ref, o_ref):
    o_ref[...] = _swiglu_update(x_ref[...], norm_ref, wa_ref, wb_ref, wo_ref)


def _ffn_ple_kernel(x_ref, p_ref, norm_ref, wa_ref, wb_ref, wo_ref, pnorm_ref, wpg_ref, wpp_ref, fnorm_ref,
                    o_ref, *, final):
    x = _swiglu_update(x_ref[...], norm_ref, wa_ref, wb_ref, wo_ref)
    gate = jax.nn.sigmoid(_dot(_rms(x, pnorm_ref[...]).astype(BF16), wpg_ref[...]))
    x = x + gate * _dot(p_ref[...].astype(BF16), wpp_ref[...])
    if final:
        x = _rms(x, fnorm_ref[...])
    o_ref[...] = x


def _ffn(x, w):
    n = x.shape[0]
    tok = lambda i: (i, 0)
    return pl.pallas_call(
        _ffn_kernel,
        grid=(n // TOKEN_TILE,),
        in_specs=[pl.BlockSpec((TOKEN_TILE, D_MODEL), tok),
                  _const_spec((1, D_MODEL)),
                  _const_spec((D_MODEL, FFN_DIM)),
                  _const_spec((D_MODEL, FFN_DIM)),
                  _const_spec((FFN_DIM, D_MODEL))],
        out_specs=pl.BlockSpec((TOKEN_TILE, D_MODEL), tok),
        out_shape=jax.ShapeDtypeStruct((n, D_MODEL), F32),
        compiler_params=_params("parallel"),
        name="ffn",
    )(x, w["ffn1_norm"], w["ffn1_wa"], w["ffn1_wb"], w["ffn1_wo"])


def _ffn_ple(x, p, layer, w, final_norm, final):
    n = x.shape[0]
    tok = lambda i: (i, 0)
    return pl.pallas_call(
        functools.partial(_ffn_ple_kernel, final=final),
        grid=(n // TOKEN_TILE,),
        in_specs=[pl.BlockSpec((TOKEN_TILE, D_MODEL), tok),
                  pl.BlockSpec((None, TOKEN_TILE, PLE_DIM), lambda i: (layer, i, 0)),
                  _const_spec((1, D_MODEL)),
                  _const_spec((D_MODEL, FFN_DIM)),
                  _const_spec((D_MODEL, FFN_DIM)),
                  _const_spec((FFN_DIM, D_MODEL)),
                  _const_spec((1, D_MODEL)),
                  _const_spec((D_MODEL, D_MODEL)),
                  _const_spec((PLE_DIM, D_MODEL)),
                  _const_spec((1, D_MODEL))],
        out_specs=pl.BlockSpec((TOKEN_TILE, D_MODEL), tok),
        out_shape=jax.ShapeDtypeStruct((n, D_MODEL), F32),
        compiler_params=_params("parallel"),
        name="ffn_ple",
    )(x, p, w["ffn2_norm"], w["ffn2_wa"], w["ffn2_wb"], w["ffn2_wo"], w["ple_norm"], w["ple_wg"], w["ple_wp"],
      final_norm)


def _qkv_kernel(x_ref, norm_ref, w_ref, k_all_ref, v_all_ref, q_ref, kb_ref, vb_ref, k_ref, v_ref):
    del k_all_ref, v_all_ref
    h = _rms(x_ref[...], norm_ref[...]).astype(BF16)
    u = _dot(h, w_ref[...])
    q_ref[...] = (u[:, :SB_WIDTH] * SB_HEAD_DIM ** -0.5).astype(BF16)
    k = u[:, SB_WIDTH:2 * SB_WIDTH]
    v = u[:, 2 * SB_WIDTH:]
    k_ref[...] = k
    v_ref[...] = v
    kb_ref[...] = k.astype(BF16)
    vb_ref[...] = v.astype(BF16)


def _qkv(x, w, k_all, v_all, layer):
    n = x.shape[0]
    tok = lambda i: (i, 0)
    layer_tok = lambda i: (layer, i, 0)
    return pl.pallas_call(
        _qkv_kernel,
        grid=(n // TOKEN_TILE,),
        in_specs=[pl.BlockSpec((TOKEN_TILE, D_MODEL), tok),
                  _const_spec((1, D_MODEL)),
                  _const_spec((D_MODEL, 3 * SB_WIDTH)),
                  pl.BlockSpec(memory_space=pl.ANY),
                  pl.BlockSpec(memory_space=pl.ANY)],
        out_specs=[pl.BlockSpec((TOKEN_TILE, SB_WIDTH), tok)] * 3
        + [pl.BlockSpec((None, TOKEN_TILE, SB_WIDTH), layer_tok)] * 2,
        out_shape=[jax.ShapeDtypeStruct((n, SB_WIDTH), BF16)] * 3
        + [jax.ShapeDtypeStruct(k_all.shape, F32), jax.ShapeDtypeStruct(v_all.shape, F32)],
        input_output_aliases={3: 3, 4: 4},
        compiler_params=_params("parallel"),
        name="qkv",
    )(x, w["mix_norm"], w["w_qkv"], k_all, v_all)


def _sb_blocks(q2, blocks, c, acc, tri_ones, head_lanes):
    pairs = len(q2)
    qb = acc[0].shape[0]
    n = tri_ones.shape[0] // 2
    log_beta, split = [], []
    for k, _, mask in blocks:
        for p in range(pairs):
            z = _dot_nt(q2[p], k[p])
            t = jnp.log(1.0 + jnp.exp(-jnp.abs(z)))
            lk = -(jnp.maximum(z, 0.0) + t)
            if mask is not None:
                lk = jnp.where(mask, lk, 0.0)
            hi = lk.astype(BF16)
            lo = (lk - hi.astype(F32)).astype(BF16)
            log_beta.append(jnp.minimum(z, 0.0) - t)
            split.append(jnp.concatenate([hi, lo], axis=1))
    x = _dot(jnp.concatenate(split, axis=0), tri_ones)
    c, acc = list(c), list(acc)
    for b, (_, v, mask) in enumerate(blocks):
        for p in range(pairs):
            r0 = (b * pairs + p) * 2 * qb
            xs = x[r0:r0 + 2 * qb]
            w = jnp.exp(log_beta[b * pairs + p] + xs[:, :n] + c[p])
            if mask is not None:
                w = jnp.where(mask, w, 0.0)
            c[p] = c[p] + xs[:, n:]
            w2 = jnp.concatenate([w[:qb], w[qb:]], axis=1).astype(BF16)
            v2 = jnp.concatenate([jnp.where(m, v[p], jnp.zeros_like(v[p])) for m in head_lanes], axis=0)
            acc[p] = acc[p] + _dot(w2, v2)
    return c, acc


def _sb_kernel(*refs, qb, n_cache, layer):
    n = SB_KEY_BLOCK
    pairs = SB_WIDTH // LANES
    lead = n - qb
    window = lead + 2 * n
    if n_cache:
        (q_ref, kn_ref, vn_ref, kc_hbm, vc_hbm, o_ref, acc_s, c_s, kwin_s, vwin_s, kold_s, vold_s,
         win_sem, old_sem) = refs
        assert n_cache >= window and kn_ref.shape[1] == qb
    else:
        q_ref, kn_ref, vn_ref, o_ref, acc_s, c_s = refs
        assert lead == 0
    bi = pl.program_id(0)
    i = pl.program_id(1)

    lane = lax.broadcasted_iota(jnp.int32, (1, LANES), 1)
    head_lanes = (lane < SB_HEAD_DIM, lane >= SB_HEAD_DIM)
    tri_row = lax.broadcasted_iota(jnp.int32, (2 * n, 2 * n), 0) & (n - 1)
    tri_col = lax.broadcasted_iota(jnp.int32, (2 * n, 2 * n), 1)
    tri_ones = jnp.logical_or(tri_col >= n, tri_row > tri_col).astype(BF16)
    key_s = lax.broadcasted_iota(jnp.int32, (1, n), 1)
    row_t = lax.broadcasted_iota(jnp.int32, (2 * qb, 1), 0) & (qb - 1)
    causal = key_s < row_t + lead

    def lanes(p):
        return slice(p * LANES, (p + 1) * LANES)

    def per_pair(ref, lead_idx, rows):
        return [ref[lead_idx, rows, lanes(p)].astype(BF16) for p in range(pairs)]

    def cache_copies(batch, start, rows, k_dst, v_dst, k_sem, v_sem):
        copies = []
        for src, dst, sem in ((kc_hbm, k_dst, k_sem), (vc_hbm, v_dst, v_sem)):
            for h in range(SB_HEADS):
                copies.append(pltpu.make_async_copy(src.at[layer, batch, pl.ds(start, rows), h, :], dst.at[h], sem))
        return copies

    def heads_to_pairs(ref, rows):
        return [jnp.concatenate([ref[2 * p, rows, :], ref[2 * p + 1, rows, :]], axis=1).astype(BF16)
                for p in range(pairs)]

    def window_copies(batch, slot):
        return cache_copies(batch, n_cache - window, window, kwin_s.at[slot], vwin_s.at[slot],
                            win_sem.at[0, slot], win_sem.at[1, slot])

    if n_cache:
        slot = bi % 2

        @pl.when(bi == 0)
        def _():
            for cp in window_copies(0, 0):
                cp.start()

        @pl.when(bi + 1 < pl.num_programs(0))
        def _():
            for cp in window_copies(bi + 1, 1 - slot):
                cp.start()

        for cp in window_copies(bi, slot):
            cp.wait()

    def stacked_q():
        q2 = []
        for p in range(pairs):
            qp = q_ref[0, :, lanes(p)]
            q2.append(jnp.concatenate([jnp.where(m, qp, jnp.zeros_like(qp)) for m in head_lanes], axis=0))
        return q2

    def first_block():
        own = pl.ds(pl.multiple_of(i * qb, qb), qb)
        ks, vs = per_pair(kn_ref, 0, own), per_pair(vn_ref, 0, own)
        if lead:
            newest = slice(window - lead, window)
            ks = [jnp.concatenate([o, x], axis=0) for o, x in zip(heads_to_pairs(kwin_s.at[slot], newest), ks)]
            vs = [jnp.concatenate([o, x], axis=0) for o, x in zip(heads_to_pairs(vwin_s.at[slot], newest), vs)]
        return ks, vs, causal

    def next_block(d):
        if n_cache:
            rows = slice(window - lead - d * n, window - lead - (d - 1) * n)
            return heads_to_pairs(kwin_s.at[slot], rows), heads_to_pairs(vwin_s.at[slot], rows), None
        rows = pl.ds(pl.multiple_of((i - d) * n, n), n)
        return per_pair(kn_ref, 0, rows), per_pair(vn_ref, 0, rows), None

    def older_block(start, mask):
        if n_cache:
            copies = cache_copies(bi, start, n, kold_s, vold_s, old_sem.at[0], old_sem.at[1])
            for cp in copies:
                cp.start()
            for cp in copies:
                cp.wait()
            return heads_to_pairs(kold_s, slice(None)), heads_to_pairs(vold_s, slice(None)), mask
        rows = pl.ds(start, n)
        return per_pair(kn_ref, 0, rows), per_pair(vn_ref, 0, rows), mask

    def run(blocks, fresh_start):
        if fresh_start:
            c = [jnp.zeros((2 * qb, n), F32)] * pairs
            acc = [jnp.zeros((qb, LANES), F32)] * pairs
        else:
            c = [c_s[p] for p in range(pairs)]
            acc = [acc_s[p] for p in range(pairs)]
        c, acc = _sb_blocks(stacked_q(), blocks, c, acc, tri_ones, head_lanes)
        for p in range(pairs):
            c_s[p], acc_s[p] = c[p], acc[p]

    def first_three():
        run([first_block(), next_block(1), next_block(2)], True)

    if n_cache:
        first_three()
        unvisited = jnp.int32(n_cache - window)
    else:
        pl.when(i >= 2)(first_three)
        pl.when(i < 2)(lambda: run([first_block()], True))
        unvisited = jnp.where(i >= 2, i - 2, i) * qb

    def live(rem):
        return jnp.logical_and(rem > 0, jnp.max(c_s[...]) >= SB_LOG_ZERO)

    def one_block(rem):
        start = pl.multiple_of(jnp.maximum(rem - n, 0), qb)
        fresh = start + key_s < rem
        run([older_block(start, fresh)], False)
        return start

    lax.while_loop(live, one_block, unvisited)
    o_ref[0] = jnp.concatenate([acc_s[p] for p in range(pairs)], axis=1).astype(o_ref.dtype)


def _sb_attention(q, k_new, v_new, k_cache, v_cache, layer):
    b, t, _ = q.shape
    n_cache = 0 if k_cache is None else k_cache.shape[2]
    qb = min(SB_QUERY_BLOCK, t)
    assert t % qb == 0 and (n_cache == 0 or t == qb)
    n = SB_KEY_BLOCK
    pairs = SB_WIDTH // LANES
    blk = lambda bi, i: (bi, i, 0)
    full = lambda bi, i: (bi, 0, 0)
    in_specs = [pl.BlockSpec((1, qb, SB_WIDTH), blk),
                pl.BlockSpec((1, t, SB_WIDTH), full),
                pl.BlockSpec((1, t, SB_WIDTH), full)]
    args = [q, k_new, v_new]
    scratch = [pltpu.VMEM((pairs, qb, LANES), F32), pltpu.VMEM((pairs, 2 * qb, n), F32)]
    if n_cache:
        window = 3 * n - qb
        in_specs += [pl.BlockSpec(memory_space=pl.ANY)] * 2
        args += [k_cache, v_cache]
        scratch += [pltpu.VMEM((2, SB_HEADS, window, SB_HEAD_DIM), F32), pltpu.VMEM((2, SB_HEADS, window, SB_HEAD_DIM), F32),
                    pltpu.VMEM((SB_HEADS, n, SB_HEAD_DIM), F32), pltpu.VMEM((SB_HEADS, n, SB_HEAD_DIM), F32),
                    pltpu.SemaphoreType.DMA((2, 2)), pltpu.SemaphoreType.DMA((2,))]
    return pl.pallas_call(
        functools.partial(_sb_kernel, qb=qb, n_cache=n_cache, layer=layer),
        grid=(b, t // qb),
        in_specs=in_specs,
        out_specs=pl.BlockSpec((1, qb, SB_WIDTH), blk),
        out_shape=jax.ShapeDtypeStruct((b, t, SB_WIDTH), BF16),
        scratch_shapes=scratch,
        compiler_params=_params("arbitrary" if n_cache else "parallel", "arbitrary"),
        name="sb_attention",
    )(*args)


def _gla_chunk(q, k, v, g, st, masks):
    tril, causal, k_heads, v_heads, st_heads = masks
    g_hi = g.astype(BF16)
    g_lo = (g - g_hi.astype(F32)).astype(BF16)
    b = _dot(tril, g_hi) + _dot(tril, g_lo)
    mid = b[GLA_CHUNK // 2 - 1:GLA_CHUNK // 2, :]
    last = b[GLA_CHUNK - 1:GLA_CHUNK, :]
    q_in = (q * jnp.exp(b - mid)).astype(BF16)
    k_in = k * jnp.exp(mid - b)
    k_rows = (jnp.concatenate([k_in] * GLA_HEADS, axis=0) * k_heads).astype(BF16)
    v_bf = v.astype(BF16)
    v_rows = jnp.concatenate([v_bf] * GLA_HEADS, axis=0) * v_heads
    scores = jnp.where(causal, _dot_nt(q_in, k_rows), 0.0).astype(BF16)
    intra = _dot(scores, v_rows)
    inter = _dot_nt((q * jnp.exp(b)).astype(BF16), st.astype(BF16))
    k_out = (k * jnp.exp(last - b)).astype(BF16)
    st_new = st * jnp.exp(last) + _dot_tn(v_bf, k_out) * st_heads
    return inter + intra, st_new


def _mixer_kernel(x_ref, ya_ref, st0_ref, hist0_ref, norm_ref, wp_ref, wgate_ref, glawg_ref, glabg_ref, glanorm_ref,
                  poolw_ref, poolscale_ref, wa_ref, wb_ref, wc_ref, wout_ref,
                  o_ref, st_ref, hist_ref, st_s, ext_s, *, seqs, rows, n_past):
    t = pl.program_id(1)
    m = seqs * rows

    @pl.when(t == 0)
    def _():
        st_s[...] = st0_ref[...]
        ext_s[:, 0:POOL_HIST_ROWS, :] = hist0_ref[...]

    x = x_ref[...].reshape(m, D_MODEL)
    h = _rms(x, norm_ref[...]).astype(BF16)
    u = _dot(h, wp_ref[...])
    kw, vw = GLA_KEY_WIDTH, GLA_VAL_WIDTH
    q_b = u[:, 0:kw] * GLA_KEY_DIM ** -0.5
    k_b = u[:, kw:2 * kw]
    v_b = u[:, 2 * kw:2 * kw + vw]
    r_b = u[:, 2 * kw + vw:3 * kw + vw]
    o_b = u[:, 3 * kw + vw:3 * kw + 2 * vw]
    u_c = u[:, 3 * kw + 2 * vw:]

    gate = _dot(r_b.astype(BF16), glawg_ref[...]) + glabg_ref[...]
    log_alpha = (jnp.minimum(gate, 0.0) - jnp.log1p(jnp.exp(-jnp.abs(gate)))) / GLA_GATE_TAU
    c = GLA_CHUNK
    ri = lax.broadcasted_iota(jnp.int32, (c, c), 0)
    ci = lax.broadcasted_iota(jnp.int32, (c, c), 1)
    tril = (ci <= ri).astype(BF16)
    key_head = _iota_div((1, kw), 1, GLA_KEY_DIM)
    val_head = _iota_div((1, vw), 1, GLA_VAL_DIM)
    rows_hs = _iota_div((GLA_HEADS * c, 1), 0, c)
    causal = (lax.broadcasted_iota(jnp.int32, (c, GLA_HEADS * c), 1) & (c - 1)) <= lax.broadcasted_iota(
        jnp.int32, (c, GLA_HEADS * c), 0)
    k_heads = (rows_hs == key_head).astype(F32)
    v_heads = (rows_hs == val_head).astype(BF16)
    st_heads = (_iota_div((vw, 1), 0, GLA_VAL_DIM) == key_head).astype(F32)
    masks = (tril, causal, k_heads, v_heads, st_heads)
    o_parts = []
    for s in range(seqs):
        st = st_s[s]
        for j in range(rows // c):
            r0 = s * rows + j * c
            o, st = _gla_chunk(q_b[r0:r0 + c], k_b[r0:r0 + c], v_b[r0:r0 + c], log_alpha[r0:r0 + c], st, masks)
            o_parts.append(o)
        st_s[s] = st
    o = jnp.concatenate(o_parts, axis=0)
    head_mean = (_iota_div((vw, 1), 0, GLA_VAL_DIM) == val_head).astype(BF16) * (1.0 / GLA_VAL_DIM)
    ms = _dot_split(o * o, head_mean)
    y_b = o * lax.rsqrt(ms + RMS_EPS) * glanorm_ref[...] * (o_b * jax.nn.sigmoid(o_b))

    lane = lax.broadcasted_iota(jnp.int32, (1, POOL_WIDTH), 1)
    pos1 = n_past + t * rows + 1 + lax.broadcasted_iota(jnp.int32, (rows, 1), 0)
    d_parts = []
    for s in range(seqs):
        ext_s[s, POOL_HIST_ROWS:, :] = u_c[s * rows:(s + 1) * rows]
        e = ext_s[s]
        sums = []
        span = 1
        for w in POOL_WINDOWS:
            while span < w:
                e = e + pltpu.roll(e, span, 0)
                span *= 2
            sums.append(e[POOL_HIST_ROWS:])
        wsum = sums[-1]
        cnt = jnp.minimum(POOL_WINDOWS[-1], pos1)
        for gi in range(len(POOL_WINDOWS) - 2, -1, -1):
            in_group = lane < (gi + 1) * POOL_GROUP_DIM
            wsum = jnp.where(in_group, sums[gi], wsum)
            cnt = jnp.where(in_group, jnp.minimum(POOL_WINDOWS[gi], pos1), cnt)
        d_parts.append(wsum / cnt.astype(F32) - u_c[s * rows:(s + 1) * rows])
        ext_s[s, 0:POOL_HIST_ROWS, :] = ext_s[s, rows:rows + POOL_HIST_ROWS, :]
    d = jnp.concatenate(d_parts, axis=0)
    y_c = _dot(d.astype(BF16), poolw_ref[...]) * poolscale_ref[...]

    branches = ((ya_ref[...].reshape(m, SB_WIDTH), wa_ref), (y_b.astype(BF16), wb_ref), (y_c.astype(BF16), wc_ref))
    merged = None
    for bi, (y, w_ref) in enumerate(branches):
        gate_b = jax.nn.sigmoid(_dot(h, wgate_ref[:, bi * D_MODEL:(bi + 1) * D_MODEL]))
        term = gate_b * _dot(y, w_ref[...])
        merged = term if merged is None else merged + term
    o_ref[...] = (x + _dot(merged.astype(BF16), wout_ref[...])).reshape(seqs, rows, D_MODEL)

    @pl.when(t == pl.num_programs(1) - 1)
    def _():
        st_ref[...] = st_s[...]
        hist_ref[...] = ext_s[:, 0:POOL_HIST_ROWS, :]


def _mixer(x, y_a, st0, hist0, w, n_past):
    b, t, _ = x.shape
    rows = min(TOKEN_TILE, t)
    seqs = TOKEN_TILE // rows
    assert t % rows == 0 and b % seqs == 0 and rows % GLA_CHUNK == 0
    tok = lambda bi, ti: (bi, ti, 0)
    per_seq = lambda bi, ti: (bi, 0, 0)
    kw, vw = GLA_KEY_WIDTH, GLA_VAL_WIDTH
    return pl.pallas_call(
        functools.partial(_mixer_kernel, seqs=seqs, rows=rows, n_past=n_past),
        grid=(b // seqs, t // rows),
        in_specs=[pl.BlockSpec((seqs, rows, D_MODEL), tok),
                  pl.BlockSpec((seqs, rows, SB_WIDTH), tok),
                  pl.BlockSpec((seqs, vw, kw), per_seq),
                  pl.BlockSpec((seqs, POOL_HIST_ROWS, POOL_WIDTH), per_seq),
                  _const_spec((1, D_MODEL)),
                  _const_spec((D_MODEL, 3 * kw + 3 * vw)),
                  _const_spec((D_MODEL, 3 * D_MODEL)),
                  _const_spec((kw, kw)),
                  _const_spec((1, kw)),
                  _const_spec((1, vw)),
                  _const_spec((POOL_WIDTH, POOL_WIDTH)),
                  _const_spec((1, POOL_WIDTH)),
                  _const_spec((SB_WIDTH, D_MODEL)),
                  _const_spec((vw, D_MODEL)),
                  _const_spec((POOL_WIDTH, D_MODEL)),
                  _const_spec((D_MODEL, D_MODEL))],
        out_specs=[pl.BlockSpec((seqs, rows, D_MODEL), tok),
                   pl.BlockSpec((seqs, vw, kw), per_seq),
                   pl.BlockSpec((seqs, POOL_HIST_ROWS, POOL_WIDTH), per_seq)],
        out_shape=[jax.ShapeDtypeStruct((b, t, D_MODEL), F32),
                   jax.ShapeDtypeStruct((b, vw, kw), F32),
                   jax.ShapeDtypeStruct((b, POOL_HIST_ROWS, POOL_WIDTH), F32)],
        scratch_shapes=[pltpu.VMEM((seqs, vw, kw), F32),
                        pltpu.VMEM((seqs, POOL_HIST_ROWS + rows, POOL_WIDTH), F32)],
        compiler_params=_params("parallel", "arbitrary"),
        name="mixer",
    )(x, y_a, st0, hist0, w["mix_norm"], w["w_proj"], w["w_gate"], w["gla_wg"], w["gla_bg"], w["gla_norm"],
      w["pool_w"], w["pool_scale"], w["w_branch_a"], w["w_branch_b"], w["w_branch_c"], w["w_out"])


def _layer_weights(i, p):
    row = lambda a: a[i].reshape(1, -1).astype(F32)
    bf = lambda a: a.astype(BF16)
    kw, vw = GLA_KEY_WIDTH, GLA_VAL_WIDTH
    w_in = p["w_in"][i]
    a_end = 3 * SB_WIDTH
    qkv_b = w_in[:, a_end:a_end + 2 * kw + vw]
    r0 = a_end + 2 * kw + vw
    r_b = jnp.pad(w_in[:, r0:r0 + GLA_GATE_RANK], ((0, 0), (0, kw - GLA_GATE_RANK)))
    o0 = r0 + GLA_GATE_RANK
    rest = w_in[:, o0:o0 + vw + POOL_WIDTH]
    g0 = o0 + vw + POOL_WIDTH
    pool_w = jnp.zeros((POOL_WIDTH, POOL_WIDTH), F32)
    for gi in range(len(POOL_WINDOWS)):
        lo = gi * POOL_GROUP_DIM
        pool_w = pool_w.at[lo:lo + POOL_GROUP_DIM, lo:lo + POOL_GROUP_DIM].set(p["pool_w"][i, gi])
    return dict(
        ffn1_norm=row(p["ffn1_norm"]), ffn1_wa=bf(p["ffn1_w_in"][i, :, :FFN_DIM]), ffn1_wb=bf(p["ffn1_w_in"][i, :, FFN_DIM:]),
        ffn1_wo=bf(p["ffn1_w_out"][i]),
        mix_norm=row(p["mix_norm"]), w_qkv=bf(w_in[:, :a_end]),
        w_proj=bf(jnp.concatenate([qkv_b, r_b, rest], axis=1)), w_gate=bf(w_in[:, g0:]),
        gla_wg=bf(jnp.pad(p["gla_w_gate"][i], ((0, kw - GLA_GATE_RANK), (0, 0)))),
        gla_bg=row(p["gla_b_gate"]), gla_norm=row(p["gla_norm"]),
        pool_w=bf(pool_w), pool_scale=row(p["pool_scale"]),
        w_branch_a=bf(p["w_branch_a"][i]), w_branch_b=bf(p["w_branch_b"][i]), w_branch_c=bf(p["w_branch_c"][i]),
        w_out=bf(p["w_out"][i]),
        ffn2_norm=row(p["ffn2_norm"]), ffn2_wa=bf(p["ffn2_w_in"][i, :, :FFN_DIM]), ffn2_wb=bf(p["ffn2_w_in"][i, :, FFN_DIM:]),
        ffn2_wo=bf(p["ffn2_w_out"][i]),
        ple_norm=row(p["ple_norm"]), ple_wg=bf(p["ple_w_gate"][i]), ple_wp=bf(p["ple_w_proj"][i]),
    )


def _state_to_blocks(s):
    b = s.shape[0]
    eye = jnp.eye(GLA_HEADS, dtype=s.dtype)
    blocks = jnp.einsum("bhde,hg->bhegd", s, eye)
    return blocks.reshape(b, GLA_VAL_WIDTH, GLA_KEY_WIDTH)


def _blocks_to_state(st):
    b = st.shape[0]
    blocks = st.reshape(b, GLA_HEADS, GLA_VAL_DIM, GLA_HEADS, GLA_KEY_DIM)
    diag = jnp.stack([blocks[:, h, :, h, :] for h in range(GLA_HEADS)], axis=1)
    return diag.transpose(0, 1, 3, 2)


def _run_group(x, p, k_cache, v_cache, s_gla, s_pool, layers, final_norm):
    b, t, _ = x.shape
    n = b * t
    depth = len(layers)
    has_cache = k_cache is not None
    n_past = k_cache.shape[2] if has_cache else 0
    x = x.reshape(n, D_MODEL)
    k_all = jnp.zeros((depth, n, SB_WIDTH), F32)
    v_all = jnp.zeros((depth, n, SB_WIDTH), F32)
    ss, ps = [], []
    for i, w in enumerate(layers):
        x = _ffn(x, w)
        q, k, v, k_all, v_all = _qkv(x, w, k_all, v_all, i)
        q, k3, v3 = (a.reshape(b, t, SB_WIDTH) for a in (q, k, v))
        if has_cache:
            st0 = _state_to_blocks(s_gla[i])
            hist0 = jnp.pad(s_pool[i], ((0, 0), (POOL_HIST_ROWS - POOL_HIST, 0), (0, 0)))
        else:
            st0 = jnp.zeros((b, GLA_VAL_WIDTH, GLA_KEY_WIDTH), F32)
            hist0 = jnp.zeros((b, POOL_HIST_ROWS, POOL_WIDTH), F32)
        y_a = _sb_attention(q, k3, v3, k_cache, v_cache, i)
        x3, st, hist = _mixer(x.reshape(b, t, D_MODEL), y_a, st0, hist0, w, n_past)
        x = _ffn_ple(x3.reshape(n, D_MODEL), p.reshape(depth, n, PLE_DIM), i, w, final_norm, final=(i == depth - 1))
        ss.append(_blocks_to_state(st))
        ps.append(hist[:, POOL_HIST_ROWS - POOL_HIST:, :])
    heads = (depth, b, t, SB_HEADS, SB_HEAD_DIM)
    return x.reshape(b, t, D_MODEL), k_all.reshape(heads), v_all.reshape(heads), jnp.stack(ss), jnp.stack(ps)


def kernel(x_prompt, x_sample, cache_sb_k, cache_sb_v, state_gla, state_pool, p_prompt, p_sample, ffn1_norm, ffn1_w_in, ffn1_w_out, mix_norm, w_in, gla_w_gate, gla_b_gate, gla_norm, pool_w, pool_scale, w_branch_a, w_branch_b, w_branch_c, w_out, ffn2_norm, ffn2_w_in, ffn2_w_out, ple_norm, ple_w_gate, ple_w_proj, final_norm):
    params = dict(ffn1_norm=ffn1_norm, ffn1_w_in=ffn1_w_in, ffn1_w_out=ffn1_w_out, mix_norm=mix_norm, w_in=w_in,
                  gla_w_gate=gla_w_gate, gla_b_gate=gla_b_gate, gla_norm=gla_norm, pool_w=pool_w, pool_scale=pool_scale,
                  w_branch_a=w_branch_a, w_branch_b=w_branch_b, w_branch_c=w_branch_c, w_out=w_out,
                  ffn2_norm=ffn2_norm, ffn2_w_in=ffn2_w_in, ffn2_w_out=ffn2_w_out, ple_norm=ple_norm,
                  ple_w_gate=ple_w_gate, ple_w_proj=ple_w_proj)
    depth = w_in.shape[0]
    layers = [_layer_weights(i, params) for i in range(depth)]
    fnorm = final_norm.reshape(1, D_MODEL).astype(F32)
    prompt = _run_group(x_prompt, p_prompt, None, None, None, None, layers, fnorm)
    sample = _run_group(x_sample, p_sample, cache_sb_k, cache_sb_v, state_gla, state_pool, layers, fnorm)
    return (prompt[0], sample[0], prompt[1], prompt[2], prompt[3], prompt[4],
            sample[1], sample[2], sample[3], sample[4])
```

```python
import functools

import jax
import jax.numpy as jnp
from jax import lax
from jax.experimental import pallas as pl
from jax.experimental.pallas import tpu as pltpu

D_MODEL = 1024
FFN_DIM = 2816
PLE_DIM = 256
SB_HEADS = 8
SB_HEAD_DIM = 64
SB_WIDTH = SB_HEADS * SB_HEAD_DIM
GLA_HEADS = 4
GLA_KEY_DIM = 32
GLA_VAL_DIM = 64
GLA_KEY_WIDTH = GLA_HEADS * GLA_KEY_DIM
GLA_VAL_WIDTH = GLA_HEADS * GLA_VAL_DIM
GLA_GATE_RANK = 16
GLA_GATE_TAU = 16.0
GLA_CHUNK = 64
POOL_WINDOWS = (2, 4, 8, 16)
POOL_GROUP_DIM = 64
POOL_WIDTH = len(POOL_WINDOWS) * POOL_GROUP_DIM
POOL_HIST = 15
POOL_HIST_ROWS = 16
RMS_EPS = 1e-6

LANES = 128
VMEM_LIMIT_BYTES = 56 * 1024 * 1024
TOKEN_TILE = 512
FFN_CHUNK = 256
SB_QUERY_BLOCK = 128
SB_KEY_BLOCK = 128
SB_LOG_ZERO = -104.0

F32 = jnp.float32
BF16 = jnp.bfloat16


def _dot(a, b):
    return jnp.dot(a, b, preferred_element_type=F32)


def _dot_nt(a, b):
    return lax.dot_general(a, b, (((1,), (1,)), ((), ())), preferred_element_type=F32)


def _dot_tn(a, b):
    return lax.dot_general(a, b, (((0,), (0,)), ((), ())), preferred_element_type=F32)


def _dot_split(a, b):
    hi = a.astype(BF16)
    lo = (a - hi.astype(F32)).astype(BF16)
    return _dot(hi, b) + _dot(lo, b)


def _iota_div(shape, axis, size):
    assert size & (size - 1) == 0
    return lax.broadcasted_iota(jnp.int32, shape, axis) >> (size.bit_length() - 1)


def _rms(x, g):
    ms = jnp.mean(x * x, axis=-1, keepdims=True)
    return x * lax.rsqrt(ms + RMS_EPS) * g


def _params(*semantics):
    return pltpu.CompilerParams(dimension_semantics=semantics, vmem_limit_bytes=VMEM_LIMIT_BYTES)


def _const_spec(shape):
    zeros = (0,) * len(shape)
    return pl.BlockSpec(shape, lambda *_: zeros, pipeline_mode=pl.Buffered(1))


def _swiglu_update(x, norm_ref, wa_ref, wb_ref, wo_ref):
    h = _rms(x, norm_ref[...]).astype(BF16)
    acc = jnp.zeros(x.shape, F32)
    for f in range(0, FFN_DIM, FFN_CHUNK):
        a = _dot(h, wa_ref[:, f:f + FFN_CHUNK])
        b = _dot(h, wb_ref[:, f:f + FFN_CHUNK])
        g = (a * jax.nn.sigmoid(a) * b).astype(BF16)
        acc = acc + _dot(g, wo_ref[f:f + FFN_CHUNK, :])
    return x + 0.5 * acc


def _ffn_kernel(x_ref, norm_ref, wa_ref, wb_ref, wo_ref, o_ref):
    o_ref[...] = _swiglu_update(x_ref[...], norm_ref, wa_ref, wb_ref, wo_ref)


def _ffn_ple_kernel(x_ref, p_ref, norm_ref, wa_ref, wb_ref, wo_ref, pnorm_ref, wpg_ref, wpp_ref, fnorm_ref,
                    o_ref, *, final):
    x = _swiglu_update(x_ref[...], norm_ref, wa_ref, wb_ref, wo_ref)
    gate = jax.nn.sigmoid(_dot(_rms(x, pnorm_ref[...]).astype(BF16), wpg_ref[...]))
    x = x + gate * _dot(p_ref[...].astype(BF16), wpp_ref[...])
    if final:
        x = _rms(x, fnorm_ref[...])
    o_ref[...] = x


def _ffn(x, w):
    n = x.shape[0]
    tok = lambda i: (i, 0)
    return pl.pallas_call(
        _ffn_kernel,
        grid=(n // TOKEN_TILE,),
        in_specs=[pl.BlockSpec((TOKEN_TILE, D_MODEL), tok),
                  _const_spec((1, D_MODEL)),
                  _const_spec((D_MODEL, FFN_DIM)),
                  _const_spec((D_MODEL, FFN_DIM)),
                  _const_spec((FFN_DIM, D_MODEL))],
        out_specs=pl.BlockSpec((TOKEN_TILE, D_MODEL), tok),
        out_shape=jax.ShapeDtypeStruct((n, D_MODEL), F32),
        compiler_params=_params("parallel"),
        name="ffn",
    )(x, w["ffn1_norm"], w["ffn1_wa"], w["ffn1_wb"], w["ffn1_wo"])


def _ffn_ple(x, p, layer, w, final_norm, final):
    n = x.shape[0]
    tok = lambda i: (i, 0)
    return pl.pallas_call(
        functools.partial(_ffn_ple_kernel, final=final),
        grid=(n // TOKEN_TILE,),
        in_specs=[pl.BlockSpec((TOKEN_TILE, D_MODEL), tok),
                  pl.BlockSpec((None, TOKEN_TILE, PLE_DIM), lambda i: (layer, i, 0)),
                  _const_spec((1, D_MODEL)),
                  _const_spec((D_MODEL, FFN_DIM)),
                  _const_spec((D_MODEL, FFN_DIM)),
                  _const_spec((FFN_DIM, D_MODEL)),
                  _const_spec((1, D_MODEL)),
                  _const_spec((D_MODEL, D_MODEL)),
                  _const_spec((PLE_DIM, D_MODEL)),
                  _const_spec((1, D_MODEL))],
        out_specs=pl.BlockSpec((TOKEN_TILE, D_MODEL), tok),
        out_shape=jax.ShapeDtypeStruct((n, D_MODEL), F32),
        compiler_params=_params("parallel"),
        name="ffn_ple",
    )(x, p, w["ffn2_norm"], w["ffn2_wa"], w["ffn2_wb"], w["ffn2_wo"], w["ple_norm"], w["ple_wg"], w["ple_wp"],
      final_norm)


def _qkv_kernel(x_ref, norm_ref, w_ref, k_all_ref, v_all_ref, q_ref, kb_ref, vb_ref, k_ref, v_ref):
    del k_all_ref, v_all_ref
    h = _rms(x_ref[...], norm_ref[...]).astype(BF16)
    u = _dot(h, w_ref[...])
    q_ref[...] = (u[:, :SB_WIDTH] * SB_HEAD_DIM ** -0.5).astype(BF16)
    k = u[:, SB_WIDTH:2 * SB_WIDTH]
    v = u[:, 2 * SB_WIDTH:]
    k_ref[...] = k
    v_ref[...] = v
    kb_ref[...] = k.astype(BF16)
    vb_ref[...] = v.astype(BF16)


def _qkv(x, w, k_all, v_all, layer):
    n = x.shape[0]
    tok = lambda i: (i, 0)
    layer_tok = lambda i: (layer, i, 0)
    return pl.pallas_call(
        _qkv_kernel,
        grid=(n // TOKEN_TILE,),
        in_specs=[pl.BlockSpec((TOKEN_TILE, D_MODEL), tok),
                  _const_spec((1, D_MODEL)),
                  _const_spec((D_MODEL, 3 * SB_WIDTH)),
                  pl.BlockSpec(memory_space=pl.ANY),
                  pl.BlockSpec(memory_space=pl.ANY)],
        out_specs=[pl.BlockSpec((TOKEN_TILE, SB_WIDTH), tok)] * 3
        + [pl.BlockSpec((None, TOKEN_TILE, SB_WIDTH), layer_tok)] * 2,
        out_shape=[jax.ShapeDtypeStruct((n, SB_WIDTH), BF16)] * 3
        + [jax.ShapeDtypeStruct(k_all.shape, F32), jax.ShapeDtypeStruct(v_all.shape, F32)],
        input_output_aliases={3: 3, 4: 4},
        compiler_params=_params("parallel"),
        name="qkv",
    )(x, w["mix_norm"], w["w_qkv"], k_all, v_all)


def _sb_blocks(q2, blocks, c, acc, tri_ones, head_lanes):
    pairs = len(q2)
    qb = acc[0].shape[0]
    n = tri_ones.shape[0] // 2
    row = lax.broadcasted_iota(jnp.int32, (LANES, 1), 0)
    head_rows = (row < SB_HEAD_DIM, row >= SB_HEAD_DIM)
    log_beta, split = [], []
    for k, _, mask, transposed in blocks:
        for p in range(pairs):
            z = _dot(q2[p], k[p]) if transposed else _dot_nt(q2[p], k[p])
            t = jnp.log(1.0 + jnp.exp(-jnp.abs(z)))
            lk = -(jnp.maximum(z, 0.0) + t)
            if mask is not None:
                lk = jnp.where(mask, lk, 0.0)
            hi = lk.astype(BF16)
            lo = (lk - hi.astype(F32)).astype(BF16)
            log_beta.append(jnp.minimum(z, 0.0) - t)
            split.append(jnp.concatenate([hi, lo], axis=1))
    x = _dot(jnp.concatenate(split, axis=0), tri_ones)
    c, acc = list(c), list(acc)
    for b, (_, v, mask, transposed) in enumerate(blocks):
        for p in range(pairs):
            r0 = (b * pairs + p) * 2 * qb
            xs = x[r0:r0 + 2 * qb]
            w = jnp.exp(log_beta[b * pairs + p] + xs[:, :n] + c[p])
            if mask is not None:
                w = jnp.where(mask, w, 0.0)
            c[p] = c[p] + xs[:, n:]
            w2 = jnp.concatenate([w[:qb], w[qb:]], axis=1).astype(BF16)
            zero = jnp.zeros_like(v[p])
            if transposed:
                v2 = jnp.concatenate([jnp.where(m, v[p], zero) for m in head_rows], axis=1)
                acc[p] = acc[p] + _dot_nt(w2, v2)
            else:
                v2 = jnp.concatenate([jnp.where(m, v[p], zero) for m in head_lanes], axis=0)
                acc[p] = acc[p] + _dot(w2, v2)
    return c, acc


def _sb_kernel(*refs, qb, n_cache, layer):
    n = SB_KEY_BLOCK
    pairs = SB_WIDTH // LANES
    if n_cache:
        (q_ref, kn_ref, vn_ref, kwin_ref, vwin_ref, kc_hbm, vc_hbm, o_ref, acc_s, c_s, kold_s, vold_s,
         old_sem) = refs
        assert n_cache % n == 0 and n_cache >= 2 * n and kn_ref.shape[1] == qb
    else:
        q_ref, kn_ref, vn_ref, o_ref, acc_s, c_s = refs
        assert qb == n
    bi = pl.program_id(0)
    i = pl.program_id(1)

    lane = lax.broadcasted_iota(jnp.int32, (1, LANES), 1)
    head_lanes = (lane < SB_HEAD_DIM, lane >= SB_HEAD_DIM)
    tri_row = lax.broadcasted_iota(jnp.int32, (2 * n, 2 * n), 0) & (n - 1)
    tri_col = lax.broadcasted_iota(jnp.int32, (2 * n, 2 * n), 1)
    tri_ones = jnp.logical_or(tri_col >= n, tri_row > tri_col).astype(BF16)
    key_s = lax.broadcasted_iota(jnp.int32, (1, n), 1)
    row_t = lax.broadcasted_iota(jnp.int32, (2 * qb, 1), 0) & (qb - 1)
    causal = key_s < row_t

    def lanes(p):
        return slice(p * LANES, (p + 1) * LANES)

    def per_pair(ref, rows):
        return [ref[0, rows, lanes(p)].astype(BF16) for p in range(pairs)]

    def per_pair_t(ref, cols):
        return [ref[lanes(p), cols].astype(BF16) for p in range(pairs)]

    def stacked_q():
        q2 = []
        for p in range(pairs):
            qp = q_ref[0, :, lanes(p)]
            q2.append(jnp.concatenate([jnp.where(m, qp, jnp.zeros_like(qp)) for m in head_lanes], axis=0))
        return q2

    def first_block():
        own = pl.ds(pl.multiple_of(i * qb, qb), qb)
        ks, vs = per_pair(kn_ref, own), per_pair(vn_ref, own)
        if qb < n:
            pad = jnp.zeros((n - qb, LANES), BF16)
            ks = [jnp.concatenate([x, pad], axis=0) for x in ks]
            vs = [jnp.concatenate([x, pad], axis=0) for x in vs]
        return ks, vs, causal, False

    def next_block(d):
        if n_cache:
            cols = slice((2 - d) * n, (3 - d) * n)
            return per_pair_t(kwin_ref, cols), per_pair_t(vwin_ref, cols), None, True
        rows = pl.ds(pl.multiple_of((i - d) * n, n), n)
        return per_pair(kn_ref, rows), per_pair(vn_ref, rows), None, False

    def older_block(start, mask):
        if n_cache:
            cols = pl.ds(start, n)
            copies = [pltpu.make_async_copy(kc_hbm.at[layer, bi, :, cols], kold_s, old_sem.at[0]),
                      pltpu.make_async_copy(vc_hbm.at[layer, bi, :, cols], vold_s, old_sem.at[1])]
            for cp in copies:
                cp.start()
            for cp in copies:
                cp.wait()
            return per_pair_t(kold_s, slice(None)), per_pair_t(vold_s, slice(None)), mask, True
        rows = pl.ds(start, n)
        return per_pair(kn_ref, rows), per_pair(vn_ref, rows), mask, False

    def run(blocks, fresh_start):
        if fresh_start:
            c = [jnp.zeros((2 * qb, n), F32)] * pairs
            acc = [jnp.zeros((qb, LANES), F32)] * pairs
        else:
            c = [c_s[p] for p in range(pairs)]
            acc = [acc_s[p] for p in range(pairs)]
        c, acc = _sb_blocks(stacked_q(), blocks, c, acc, tri_ones, head_lanes)
        for p in range(pairs):
            c_s[p], acc_s[p] = c[p], acc[p]

    def first_three():
        run([first_block(), next_block(1), next_block(2)], True)

    if n_cache:
        first_three()
        unvisited = jnp.int32(n_cache - 2 * n)
    else:
        pl.when(i >= 2)(first_three)
        pl.when(i < 2)(lambda: run([first_block()], True))
        unvisited = jnp.where(i >= 2, i - 2, i) * n

    def live(rem):
        return jnp.logical_and(rem > 0, jnp.max(c_s[...]) >= SB_LOG_ZERO)

    def one_block(rem):
        start = pl.multiple_of(rem - n, n)
        run([older_block(start, None)], False)
        return start

    lax.while_loop(live, one_block, unvisited)
    o_ref[0] = jnp.concatenate([acc_s[p] for p in range(pairs)], axis=1).astype(o_ref.dtype)


def _sb_attention(q, k_new, v_new, k_cache, v_cache, layer):
    b, t, _ = q.shape
    n_cache = 0 if k_cache is None else k_cache.shape[3]
    qb = min(SB_QUERY_BLOCK, t)
    assert t % qb == 0 and (n_cache == 0 or t == qb)
    n = SB_KEY_BLOCK
    pairs = SB_WIDTH // LANES
    blk = lambda bi, i: (bi, i, 0)
    full = lambda bi, i: (bi, 0, 0)
    in_specs = [pl.BlockSpec((1, qb, SB_WIDTH), blk),
                pl.BlockSpec((1, t, SB_WIDTH), full),
                pl.BlockSpec((1, t, SB_WIDTH), full)]
    args = [q, k_new, v_new]
    scratch = [pltpu.VMEM((pairs, qb, LANES), F32), pltpu.VMEM((pairs, 2 * qb, n), F32)]
    if n_cache:
        newest = lambda bi, i: (layer, bi, 0, n_cache // (2 * n) - 1)
        in_specs += [pl.BlockSpec((None, None, SB_WIDTH, 2 * n), newest)] * 2 + [pl.BlockSpec(memory_space=pl.ANY)] * 2
        args += [k_cache, v_cache, k_cache, v_cache]
        scratch += [pltpu.VMEM((SB_WIDTH, n), F32), pltpu.VMEM((SB_WIDTH, n), F32), pltpu.SemaphoreType.DMA((2,))]
    return pl.pallas_call(
        functools.partial(_sb_kernel, qb=qb, n_cache=n_cache, layer=layer),
        grid=(b, t // qb),
        in_specs=in_specs,
        out_specs=pl.BlockSpec((1, qb, SB_WIDTH), blk),
        out_shape=jax.ShapeDtypeStruct((b, t, SB_WIDTH), BF16),
        scratch_shapes=scratch,
        compiler_params=_params("parallel", "arbitrary"),
        name="sb_attention",
    )(*args)


def _gla_chunk(q, k, v, g, st, masks):
    tril, causal, k_heads, v_heads, st_heads = masks
    g_hi = g.astype(BF16)
    g_lo = (g - g_hi.astype(F32)).astype(BF16)
    b = _dot(tril, g_hi) + _dot(tril, g_lo)
    mid = b[GLA_CHUNK // 2 - 1:GLA_CHUNK // 2, :]
    last = b[GLA_CHUNK - 1:GLA_CHUNK, :]
    q_in = (q * jnp.exp(b - mid)).astype(BF16)
    k_in = k * jnp.exp(mid - b)
    k_rows = (jnp.concatenate([k_in] * GLA_HEADS, axis=0) * k_heads).astype(BF16)
    v_bf = v.astype(BF16)
    v_rows = jnp.concatenate([v_bf] * GLA_HEADS, axis=0) * v_heads
    scores = jnp.where(causal, _dot_nt(q_in, k_rows), 0.0).astype(BF16)
    intra = _dot(scores, v_rows)
    inter = _dot_nt((q * jnp.exp(b)).astype(BF16), st.astype(BF16))
    k_out = (k * jnp.exp(last - b)).astype(BF16)
    st_new = st * jnp.exp(last) + _dot_tn(v_bf, k_out) * st_heads
    return inter + intra, st_new


def _mixer_kernel(x_ref, ya_ref, st0_ref, hist0_ref, norm_ref, wp_ref, wgate_ref, glawg_ref, glabg_ref, glanorm_ref,
                  poolw_ref, poolscale_ref, wa_ref, wb_ref, wc_ref, wout_ref,
                  o_ref, st_ref, hist_ref, st_s, ext_s, *, seqs, rows, n_past):
    t = pl.program_id(1)
    m = seqs * rows

    @pl.when(t == 0)
    def _():
        st_s[...] = st0_ref[...]
        ext_s[:, 0:POOL_HIST_ROWS, :] = hist0_ref[...]

    x = x_ref[...].reshape(m, D_MODEL)
    h = _rms(x, norm_ref[...]).astype(BF16)
    u = _dot(h, wp_ref[...])
    kw, vw = GLA_KEY_WIDTH, GLA_VAL_WIDTH
    q_b = u[:, 0:kw] * GLA_KEY_DIM ** -0.5
    k_b = u[:, kw:2 * kw]
    v_b = u[:, 2 * kw:2 * kw + vw]
    r_b = u[:, 2 * kw + vw:3 * kw + vw]
    o_b = u[:, 3 * kw + vw:3 * kw + 2 * vw]
    u_c = u[:, 3 * kw + 2 * vw:]

    gate = _dot(r_b.astype(BF16), glawg_ref[...]) + glabg_ref[...]
    log_alpha = (jnp.minimum(gate, 0.0) - jnp.log1p(jnp.exp(-jnp.abs(gate)))) / GLA_GATE_TAU
    c = GLA_CHUNK
    ri = lax.broadcasted_iota(jnp.int32, (c, c), 0)
    ci = lax.broadcasted_iota(jnp.int32, (c, c), 1)
    tril = (ci <= ri).astype(BF16)
    key_head = _iota_div((1, kw), 1, GLA_KEY_DIM)
    val_head = _iota_div((1, vw), 1, GLA_VAL_DIM)
    rows_hs = _iota_div((GLA_HEADS * c, 1), 0, c)
    causal = (lax.broadcasted_iota(jnp.int32, (c, GLA_HEADS * c), 1) & (c - 1)) <= lax.broadcasted_iota(
        jnp.int32, (c, GLA_HEADS * c), 0)
    k_heads = (rows_hs == key_head).astype(F32)
    v_heads = (rows_hs == val_head).astype(BF16)
    st_heads = (_iota_div((vw, 1), 0, GLA_VAL_DIM) == key_head).astype(F32)
    masks = (tril, causal, k_heads, v_heads, st_heads)
    o_parts = []
    for s in range(seqs):
        st = st_s[s]
        for j in range(rows // c):
            r0 = s * rows + j * c
            o, st = _gla_chunk(q_b[r0:r0 + c], k_b[r0:r0 + c], v_b[r0:r0 + c], log_alpha[r0:r0 + c], st, masks)
            o_parts.append(o)
        st_s[s] = st
    o = jnp.concatenate(o_parts, axis=0)
    head_mean = (_iota_div((vw, 1), 0, GLA_VAL_DIM) == val_head).astype(BF16) * (1.0 / GLA_VAL_DIM)
    ms = _dot_split(o * o, head_mean)
    y_b = o * lax.rsqrt(ms + RMS_EPS) * glanorm_ref[...] * (o_b * jax.nn.sigmoid(o_b))

    lane = lax.broadcasted_iota(jnp.int32, (1, POOL_WIDTH), 1)
    pos1 = n_past + t * rows + 1 + lax.broadcasted_iota(jnp.int32, (rows, 1), 0)
    d_parts = []
    for s in range(seqs):
        ext_s[s, POOL_HIST_ROWS:, :] = u_c[s * rows:(s + 1) * rows]
        e = ext_s[s]
        sums = []
        span = 1
        for w in POOL_WINDOWS:
            while span < w:
                e = e + pltpu.roll(e, span, 0)
                span *= 2
            sums.append(e[POOL_HIST_ROWS:])
        wsum = sums[-1]
        cnt = jnp.minimum(POOL_WINDOWS[-1], pos1)
        for gi in range(len(POOL_WINDOWS) - 2, -1, -1):
            in_group = lane < (gi + 1) * POOL_GROUP_DIM
            wsum = jnp.where(in_group, sums[gi], wsum)
            cnt = jnp.where(in_group, jnp.minimum(POOL_WINDOWS[gi], pos1), cnt)
        d_parts.append(wsum / cnt.astype(F32) - u_c[s * rows:(s + 1) * rows])
        ext_s[s, 0:POOL_HIST_ROWS, :] = ext_s[s, rows:rows + POOL_HIST_ROWS, :]
    d = jnp.concatenate(d_parts, axis=0)
    y_c = _dot(d.astype(BF16), poolw_ref[...]) * poolscale_ref[...]

    branches = ((ya_ref[...].reshape(m, SB_WIDTH), wa_ref), (y_b.astype(BF16), wb_ref), (y_c.astype(BF16), wc_ref))
    merged = None
    for bi, (y, w_ref) in enumerate(branches):
        gate_b = jax.nn.sigmoid(_dot(h, wgate_ref[:, bi * D_MODEL:(bi + 1) * D_MODEL]))
        term = gate_b * _dot(y, w_ref[...])
        merged = term if merged is None else merged + term
    o_ref[...] = (x + _dot(merged.astype(BF16), wout_ref[...])).reshape(seqs, rows, D_MODEL)

    @pl.when(t == pl.num_programs(1) - 1)
    def _():
        st_ref[...] = st_s[...]
        hist_ref[...] = ext_s[:, 0:POOL_HIST_ROWS, :]


def _mixer(x, y_a, st0, hist0, w, n_past):
    b, t, _ = x.shape
    rows = min(TOKEN_TILE, t)
    seqs = TOKEN_TILE // rows
    assert t % rows == 0 and b % seqs == 0 and rows % GLA_CHUNK == 0
    tok = lambda bi, ti: (bi, ti, 0)
    per_seq = lambda bi, ti: (bi, 0, 0)
    kw, vw = GLA_KEY_WIDTH, GLA_VAL_WIDTH
    return pl.pallas_call(
        functools.partial(_mixer_kernel, seqs=seqs, rows=rows, n_past=n_past),
        grid=(b // seqs, t // rows),
        in_specs=[pl.BlockSpec((seqs, rows, D_MODEL), tok),
                  pl.BlockSpec((seqs, rows, SB_WIDTH), tok),
                  pl.BlockSpec((seqs, vw, kw), per_seq),
                  pl.BlockSpec((seqs, POOL_HIST_ROWS, POOL_WIDTH), per_seq),
                  _const_spec((1, D_MODEL)),
                  _const_spec((D_MODEL, 3 * kw + 3 * vw)),
                  _const_spec((D_MODEL, 3 * D_MODEL)),
                  _const_spec((kw, kw)),
                  _const_spec((1, kw)),
                  _const_spec((1, vw)),
                  _const_spec((POOL_WIDTH, POOL_WIDTH)),
                  _const_spec((1, POOL_WIDTH)),
                  _const_spec((SB_WIDTH, D_MODEL)),
                  _const_spec((vw, D_MODEL)),
                  _const_spec((POOL_WIDTH, D_MODEL)),
                  _const_spec((D_MODEL, D_MODEL))],
        out_specs=[pl.BlockSpec((seqs, rows, D_MODEL), tok),
                   pl.BlockSpec((seqs, vw, kw), per_seq),
                   pl.BlockSpec((seqs, POOL_HIST_ROWS, POOL_WIDTH), per_seq)],
        out_shape=[jax.ShapeDtypeStruct((b, t, D_MODEL), F32),
                   jax.ShapeDtypeStruct((b, vw, kw), F32),
                   jax.ShapeDtypeStruct((b, POOL_HIST_ROWS, POOL_WIDTH), F32)],
        scratch_shapes=[pltpu.VMEM((seqs, vw, kw), F32),
                        pltpu.VMEM((seqs, POOL_HIST_ROWS + rows, POOL_WIDTH), F32)],
        compiler_params=_params("parallel", "arbitrary"),
        name="mixer",
    )(x, y_a, st0, hist0, w["mix_norm"], w["w_proj"], w["w_gate"], w["gla_wg"], w["gla_bg"], w["gla_norm"],
      w["pool_w"], w["pool_scale"], w["w_branch_a"], w["w_branch_b"], w["w_branch_c"], w["w_out"])


def _layer_weights(i, p):
    row = lambda a: a[i].reshape(1, -1).astype(F32)
    bf = lambda a: a.astype(BF16)
    kw, vw = GLA_KEY_WIDTH, GLA_VAL_WIDTH
    w_in = p["w_in"][i]
    a_end = 3 * SB_WIDTH
    qkv_b = w_in[:, a_end:a_end + 2 * kw + vw]
    r0 = a_end + 2 * kw + vw
    r_b = jnp.pad(w_in[:, r0:r0 + GLA_GATE_RANK], ((0, 0), (0, kw - GLA_GATE_RANK)))
    o0 = r0 + GLA_GATE_RANK
    rest = w_in[:, o0:o0 + vw + POOL_WIDTH]
    g0 = o0 + vw + POOL_WIDTH
    pool_w = jnp.zeros((POOL_WIDTH, POOL_WIDTH), F32)
    for gi in range(len(POOL_WINDOWS)):
        lo = gi * POOL_GROUP_DIM
        pool_w = pool_w.at[lo:lo + POOL_GROUP_DIM, lo:lo + POOL_GROUP_DIM].set(p["pool_w"][i, gi])
    return dict(
        ffn1_norm=row(p["ffn1_norm"]), ffn1_wa=bf(p["ffn1_w_in"][i, :, :FFN_DIM]), ffn1_wb=bf(p["ffn1_w_in"][i, :, FFN_DIM:]),
        ffn1_wo=bf(p["ffn1_w_out"][i]),
        mix_norm=row(p["mix_norm"]), w_qkv=bf(w_in[:, :a_end]),
        w_proj=bf(jnp.concatenate([qkv_b, r_b, rest], axis=1)), w_gate=bf(w_in[:, g0:]),
        gla_wg=bf(jnp.pad(p["gla_w_gate"][i], ((0, kw - GLA_GATE_RANK), (0, 0)))),
        gla_bg=row(p["gla_b_gate"]), gla_norm=row(p["gla_norm"]),
        pool_w=bf(pool_w), pool_scale=row(p["pool_scale"]),
        w_branch_a=bf(p["w_branch_a"][i]), w_branch_b=bf(p["w_branch_b"][i]), w_branch_c=bf(p["w_branch_c"][i]),
        w_out=bf(p["w_out"][i]),
        ffn2_norm=row(p["ffn2_norm"]), ffn2_wa=bf(p["ffn2_w_in"][i, :, :FFN_DIM]), ffn2_wb=bf(p["ffn2_w_in"][i, :, FFN_DIM:]),
        ffn2_wo=bf(p["ffn2_w_out"][i]),
        ple_norm=row(p["ple_norm"]), ple_wg=bf(p["ple_w_gate"][i]), ple_wp=bf(p["ple_w_proj"][i]),
    )


def _state_to_blocks(s):
    b = s.shape[0]
    eye = jnp.eye(GLA_HEADS, dtype=s.dtype)
    blocks = jnp.einsum("bhde,hg->bhegd", s, eye)
    return blocks.reshape(b, GLA_VAL_WIDTH, GLA_KEY_WIDTH)


def _blocks_to_state(st):
    b = st.shape[0]
    blocks = st.reshape(b, GLA_HEADS, GLA_VAL_DIM, GLA_HEADS, GLA_KEY_DIM)
    diag = jnp.stack([blocks[:, h, :, h, :] for h in range(GLA_HEADS)], axis=1)
    return diag.transpose(0, 1, 3, 2)


def _run_group(x, p, k_cache, v_cache, s_gla, s_pool, layers, final_norm):
    b, t, _ = x.shape
    n = b * t
    depth = len(layers)
    has_cache = k_cache is not None
    n_past = k_cache.shape[2] if has_cache else 0
    if has_cache:
        k_cache, v_cache = (a.transpose(0, 1, 3, 4, 2).reshape(depth, b, SB_WIDTH, n_past) for a in (k_cache, v_cache))
    x = x.reshape(n, D_MODEL)
    k_all = jnp.zeros((depth, n, SB_WIDTH), F32)
    v_all = jnp.zeros((depth, n, SB_WIDTH), F32)
    ss, ps = [], []
    for i, w in enumerate(layers):
        x = _ffn(x, w)
        q, k, v, k_all, v_all = _qkv(x, w, k_all, v_all, i)
        q, k3, v3 = (a.reshape(b, t, SB_WIDTH) for a in (q, k, v))
        if has_cache:
            st0 = _state_to_blocks(s_gla[i])
            hist0 = jnp.pad(s_pool[i], ((0, 0), (POOL_HIST_ROWS - POOL_HIST, 0), (0, 0)))
        else:
            st0 = jnp.zeros((b, GLA_VAL_WIDTH, GLA_KEY_WIDTH), F32)
            hist0 = jnp.zeros((b, POOL_HIST_ROWS, POOL_WIDTH), F32)
        y_a = _sb_attention(q, k3, v3, k_cache, v_cache, i)
        x3, st, hist = _mixer(x.reshape(b, t, D_MODEL), y_a, st0, hist0, w, n_past)
        x = _ffn_ple(x3.reshape(n, D_MODEL), p.reshape(depth, n, PLE_DIM), i, w, final_norm, final=(i == depth - 1))
        ss.append(_blocks_to_state(st))
        ps.append(hist[:, POOL_HIST_ROWS - POOL_HIST:, :])
    heads = (depth, b, t, SB_HEADS, SB_HEAD_DIM)
    return x.reshape(b, t, D_MODEL), k_all.reshape(heads), v_all.reshape(heads), jnp.stack(ss), jnp.stack(ps)


def kernel(x_prompt, x_sample, cache_sb_k, cache_sb_v, state_gla, state_pool, p_prompt, p_sample, ffn1_norm, ffn1_w_in, ffn1_w_out, mix_norm, w_in, gla_w_gate, gla_b_gate, gla_norm, pool_w, pool_scale, w_branch_a, w_branch_b, w_branch_c, w_out, ffn2_norm, ffn2_w_in, ffn2_w_out, ple_norm, ple_w_gate, ple_w_proj, final_norm):
    params = dict(ffn1_norm=ffn1_norm, ffn1_w_in=ffn1_w_in, ffn1_w_out=ffn1_w_out, mix_norm=mix_norm, w_in=w_in,
                  gla_w_gate=gla_w_gate, gla_b_gate=gla_b_gate, gla_norm=gla_norm, pool_w=pool_w, pool_scale=pool_scale,
                  w_branch_a=w_branch_a, w_branch_b=w_branch_b, w_branch_c=w_branch_c, w_out=w_out,
                  ffn2_norm=ffn2_norm, ffn2_w_in=ffn2_w_in, ffn2_w_out=ffn2_w_out, ple_norm=ple_norm,
                  ple_w_gate=ple_w_gate, ple_w_proj=ple_w_proj)
    depth = w_in.shape[0]
    layers = [_layer_weights(i, params) for i in range(depth)]
    fnorm = final_norm.reshape(1, D_MODEL).astype(F32)
    prompt = _run_group(x_prompt, p_prompt, None, None, None, None, layers, fnorm)
    sample = _run_group(x_sample, p_sample, cache_sb_k, cache_sb_v, state_gla, state_pool, layers, fnorm)
    return (prompt[0], sample[0], prompt[1], prompt[2], prompt[3], prompt[4],
            sample[1], sample[2], sample[3], sample[4])
```

```python
import functools

import jax
import jax.numpy as jnp
from jax import lax
from jax.experimental import pallas as pl
from jax.experimental.pallas import tpu as pltpu

D_MODEL = 1024
FFN_DIM = 2816
PLE_DIM = 256
SB_HEADS = 8
SB_HEAD_DIM = 64
SB_WIDTH = SB_HEADS * SB_HEAD_DIM
GLA_HEADS = 4
GLA_KEY_DIM = 32
GLA_VAL_DIM = 64
GLA_KEY_WIDTH = GLA_HEADS * GLA_KEY_DIM
GLA_VAL_WIDTH = GLA_HEADS * GLA_VAL_DIM
GLA_GATE_RANK = 16
GLA_GATE_TAU = 16.0
GLA_CHUNK = 64
POOL_WINDOWS = (2, 4, 8, 16)
POOL_GROUP_DIM = 64
POOL_WIDTH = len(POOL_WINDOWS) * POOL_GROUP_DIM
POOL_HIST = 15
POOL_HIST_ROWS = 16
RMS_EPS = 1e-6

LANES = 128
VMEM_LIMIT_BYTES = 56 * 1024 * 1024
TOKEN_TILE = 512
FFN_CHUNK = 256
SB_QUERY_BLOCK = 128
SB_KEY_BLOCK = 128
SB_LOG_ZERO = -104.0

F32 = jnp.float32
BF16 = jnp.bfloat16


def _dot(a, b):
    return jnp.dot(a, b, preferred_element_type=F32)


def _dot_nt(a, b):
    return lax.dot_general(a, b, (((1,), (1,)), ((), ())), preferred_element_type=F32)


def _dot_tn(a, b):
    return lax.dot_general(a, b, (((0,), (0,)), ((), ())), preferred_element_type=F32)


def _dot_split(a, b):
    hi = a.astype(BF16)
    lo = (a - hi.astype(F32)).astype(BF16)
    return _dot(hi, b) + _dot(lo, b)


def _iota_div(shape, axis, size):
    assert size & (size - 1) == 0
    return lax.broadcasted_iota(jnp.int32, shape, axis) >> (size.bit_length() - 1)


def _rms(x, g):
    ms = jnp.mean(x * x, axis=-1, keepdims=True)
    return x * lax.rsqrt(ms + RMS_EPS) * g


def _params(*semantics):
    return pltpu.CompilerParams(dimension_semantics=semantics, vmem_limit_bytes=VMEM_LIMIT_BYTES)


def _const_spec(shape):
    zeros = (0,) * len(shape)
    return pl.BlockSpec(shape, lambda *_: zeros, pipeline_mode=pl.Buffered(1))


def _swiglu_update(x, norm_ref, wa_ref, wb_ref, wo_ref):
    h = _rms(x, norm_ref[...]).astype(BF16)
    acc = jnp.zeros(x.shape, F32)
    for f in range(0, FFN_DIM, FFN_CHUNK):
        a = _dot(h, wa_ref[:, f:f + FFN_CHUNK])
        b = _dot(h, wb_ref[:, f:f + FFN_CHUNK])
        g = (a * jax.nn.sigmoid(a) * b).astype(BF16)
        acc = acc + _dot(g, wo_ref[f:f + FFN_CHUNK, :])
    return x + 0.5 * acc


def _ffn_kernel(x_ref, norm_ref, wa_ref, wb_ref, wo_ref, o_ref):
    o_ref[...] = _swiglu_update(x_ref[...], norm_ref, wa_ref, wb_ref, wo_ref)


def _ffn_ple_kernel(x_ref, p_ref, norm_ref, wa_ref, wb_ref, wo_ref, pnorm_ref, wpg_ref, wpp_ref, fnorm_ref,
                    o_ref, *, final):
    x = _swiglu_update(x_ref[...], norm_ref, wa_ref, wb_ref, wo_ref)
    gate = jax.nn.sigmoid(_dot(_rms(x, pnorm_ref[...]).astype(BF16), wpg_ref[...]))
    x = x + gate * _dot(p_ref[...].astype(BF16), wpp_ref[...])
    if final:
        x = _rms(x, fnorm_ref[...])
    o_ref[...] = x


def _ffn(x, w):
    n = x.shape[0]
    tok = lambda i: (i, 0)
    return pl.pallas_call(
        _ffn_kernel,
        grid=(n // TOKEN_TILE,),
        in_specs=[pl.BlockSpec((TOKEN_TILE, D_MODEL), tok),
                  _const_spec((1, D_MODEL)),
                  _const_spec((D_MODEL, FFN_DIM)),
                  _const_spec((D_MODEL, FFN_DIM)),
                  _const_spec((FFN_DIM, D_MODEL))],
        out_specs=pl.BlockSpec((TOKEN_TILE, D_MODEL), tok),
        out_shape=jax.ShapeDtypeStruct((n, D_MODEL), F32),
        compiler_params=_params("parallel"),
        name="ffn",
    )(x, w["ffn1_norm"], w["ffn1_wa"], w["ffn1_wb"], w["ffn1_wo"])


def _ffn_ple(x, p, layer, w, final_norm, final):
    n = x.shape[0]
    tok = lambda i: (i, 0)
    return pl.pallas_call(
        functools.partial(_ffn_ple_kernel, final=final),
        grid=(n // TOKEN_TILE,),
        in_specs=[pl.BlockSpec((TOKEN_TILE, D_MODEL), tok),
                  pl.BlockSpec((None, TOKEN_TILE, PLE_DIM), lambda i: (layer, i, 0)),
                  _const_spec((1, D_MODEL)),
                  _const_spec((D_MODEL, FFN_DIM)),
                  _const_spec((D_MODEL, FFN_DIM)),
                  _const_spec((FFN_DIM, D_MODEL)),
                  _const_spec((1, D_MODEL)),
                  _const_spec((D_MODEL, D_MODEL)),
                  _const_spec((PLE_DIM, D_MODEL)),
                  _const_spec((1, D_MODEL))],
        out_specs=pl.BlockSpec((TOKEN_TILE, D_MODEL), tok),
        out_shape=jax.ShapeDtypeStruct((n, D_MODEL), F32),
        compiler_params=_params("parallel"),
        name="ffn_ple",
    )(x, p, w["ffn2_norm"], w["ffn2_wa"], w["ffn2_wb"], w["ffn2_wo"], w["ple_norm"], w["ple_wg"], w["ple_wp"],
      final_norm)


def _qkv_kernel(x_ref, norm_ref, w_ref, k_all_ref, v_all_ref, q_ref, kb_ref, vb_ref, k_ref, v_ref):
    del k_all_ref, v_all_ref
    h = _rms(x_ref[...], norm_ref[...]).astype(BF16)
    u = _dot(h, w_ref[...])
    q_ref[...] = (u[:, :SB_WIDTH] * SB_HEAD_DIM ** -0.5).astype(BF16)
    k = u[:, SB_WIDTH:2 * SB_WIDTH]
    v = u[:, 2 * SB_WIDTH:]
    k_ref[...] = k
    v_ref[...] = v
    kb_ref[...] = k.astype(BF16)
    vb_ref[...] = v.astype(BF16)


def _qkv(x, w, k_all, v_all, layer):
    n = x.shape[0]
    tok = lambda i: (i, 0)
    layer_tok = lambda i: (layer, i, 0)
    return pl.pallas_call(
        _qkv_kernel,
        grid=(n // TOKEN_TILE,),
        in_specs=[pl.BlockSpec((TOKEN_TILE, D_MODEL), tok),
                  _const_spec((1, D_MODEL)),
                  _const_spec((D_MODEL, 3 * SB_WIDTH)),
                  pl.BlockSpec(memory_space=pl.ANY),
                  pl.BlockSpec(memory_space=pl.ANY)],
        out_specs=[pl.BlockSpec((TOKEN_TILE, SB_WIDTH), tok)] * 3
        + [pl.BlockSpec((None, TOKEN_TILE, SB_WIDTH), layer_tok)] * 2,
        out_shape=[jax.ShapeDtypeStruct((n, SB_WIDTH), BF16)] * 3
        + [jax.ShapeDtypeStruct(k_all.shape, F32), jax.ShapeDtypeStruct(v_all.shape, F32)],
        input_output_aliases={3: 3, 4: 4},
        compiler_params=_params("parallel"),
        name="qkv",
    )(x, w["mix_norm"], w["w_qkv"], k_all, v_all)


def _sb_blocks(chains, tri_ones, head_lanes):
    n = tri_ones.shape[0] // 2
    row = lax.broadcasted_iota(jnp.int32, (LANES, 1), 0)
    head_rows = (row < SB_HEAD_DIM, row >= SB_HEAD_DIM)
    log_beta, split = [], []
    for q2, blocks, _, _ in chains:
        for k, _, mask, transposed in blocks:
            for p in range(len(q2)):
                z = _dot(q2[p], k[p]) if transposed else _dot_nt(q2[p], k[p])
                t = jnp.log(1.0 + jnp.exp(-jnp.abs(z)))
                lk = -(jnp.maximum(z, 0.0) + t)
                if mask is not None:
                    lk = jnp.where(mask, lk, 0.0)
                hi = lk.astype(BF16)
                lo = (lk - hi.astype(F32)).astype(BF16)
                log_beta.append(jnp.minimum(z, 0.0) - t)
                split.append(jnp.concatenate([hi, lo], axis=1))
    x = _dot(jnp.concatenate(split, axis=0), tri_ones)
    results = []
    visit = 0
    r0 = 0
    for q2, blocks, c, acc in chains:
        c, acc = list(c), list(acc)
        rows = acc[0].shape[0]
        for _, v, mask, transposed in blocks:
            for p in range(len(q2)):
                xs = x[r0:r0 + 2 * rows]
                w = jnp.exp(log_beta[visit] + xs[:, :n] + c[p])
                if mask is not None:
                    w = jnp.where(mask, w, 0.0)
                c[p] = c[p] + xs[:, n:]
                w2 = jnp.concatenate([w[:rows], w[rows:]], axis=1).astype(BF16)
                zero = jnp.zeros_like(v[p])
                if transposed:
                    v2 = jnp.concatenate([jnp.where(m, v[p], zero) for m in head_rows], axis=1)
                    acc[p] = acc[p] + _dot_nt(w2, v2)
                else:
                    v2 = jnp.concatenate([jnp.where(m, v[p], zero) for m in head_lanes], axis=0)
                    acc[p] = acc[p] + _dot(w2, v2)
                visit += 1
                r0 += 2 * rows
        results.append((c, acc))
    return results


def _sb_kernel(*refs, qb, n_cache, layer):
    n = SB_KEY_BLOCK
    pairs = SB_WIDTH // LANES
    if n_cache:
        (q_ref, kn_ref, vn_ref, kwin_ref, vwin_ref, kc_hbm, vc_hbm, o_ref, acc_s, c_s, kold_s, vold_s,
         old_sem) = refs
        assert n_cache % n == 0 and n_cache >= 2 * n and kn_ref.shape[1] == qb
    else:
        q_ref, kn_ref, vn_ref, o_ref, acc_s, c_s = refs
        assert qb == n
    bi = pl.program_id(0)
    i = pl.program_id(1)

    lane = lax.broadcasted_iota(jnp.int32, (1, LANES), 1)
    head_lanes = (lane < SB_HEAD_DIM, lane >= SB_HEAD_DIM)
    tri_row = lax.broadcasted_iota(jnp.int32, (2 * n, 2 * n), 0) & (n - 1)
    tri_col = lax.broadcasted_iota(jnp.int32, (2 * n, 2 * n), 1)
    tri_ones = jnp.logical_or(tri_col >= n, tri_row > tri_col).astype(BF16)
    key_s = lax.broadcasted_iota(jnp.int32, (1, n), 1)
    row_t = lax.broadcasted_iota(jnp.int32, (2 * qb, 1), 0) & (qb - 1)
    causal = key_s < row_t

    def lanes(p):
        return slice(p * LANES, (p + 1) * LANES)

    def per_pair(ref, rows):
        return [ref[0, rows, lanes(p)].astype(BF16) for p in range(pairs)]

    def per_pair_t(ref, cols):
        return [ref[lanes(p), cols].astype(BF16) for p in range(pairs)]

    def stacked_q(rows=slice(None)):
        q2 = []
        for p in range(pairs):
            qp = q_ref[0, rows, lanes(p)]
            q2.append(jnp.concatenate([jnp.where(m, qp, jnp.zeros_like(qp)) for m in head_lanes], axis=0))
        return q2

    def first_block():
        own = pl.ds(pl.multiple_of(i * qb, qb), qb)
        ks, vs = per_pair(kn_ref, own), per_pair(vn_ref, own)
        if qb < n:
            pad = jnp.zeros((n - qb, LANES), BF16)
            ks = [jnp.concatenate([x, pad], axis=0) for x in ks]
            vs = [jnp.concatenate([x, pad], axis=0) for x in vs]
        return ks, vs, causal, False

    def next_block(d):
        if n_cache:
            cols = slice((2 - d) * n, (3 - d) * n)
            return per_pair_t(kwin_ref, cols), per_pair_t(vwin_ref, cols), None, True
        rows = pl.ds(pl.multiple_of((i - d) * n, n), n)
        return per_pair(kn_ref, rows), per_pair(vn_ref, rows), None, False

    def older_block(start, mask):
        if n_cache:
            cols = pl.ds(start, n)
            copies = [pltpu.make_async_copy(kc_hbm.at[layer, bi, :, cols], kold_s, old_sem.at[0]),
                      pltpu.make_async_copy(vc_hbm.at[layer, bi, :, cols], vold_s, old_sem.at[1])]
            for cp in copies:
                cp.start()
            for cp in copies:
                cp.wait()
            return per_pair_t(kold_s, slice(None)), per_pair_t(vold_s, slice(None)), mask, True
        rows = pl.ds(start, n)
        return per_pair(kn_ref, rows), per_pair(vn_ref, rows), mask, False

    def run(blocks, fresh_start):
        if fresh_start:
            c = [jnp.zeros((2 * qb, n), F32)] * pairs
            acc = [jnp.zeros((qb, LANES), F32)] * pairs
        else:
            c = [c_s[p] for p in range(pairs)]
            acc = [acc_s[p] for p in range(pairs)]
        ((c, acc),) = _sb_blocks([(stacked_q(), blocks, c, acc)], tri_ones, head_lanes)
        for p in range(pairs):
            c_s[p], acc_s[p] = c[p], acc[p]

    half = qb // 2

    def two_halves():
        in_reach = key_s < (row_t & (half - 1))[:2 * half] + (n - half)
        chains = []
        for u in range(2):
            first = pl.multiple_of(i * qb + (u + 1) * half - n, half)
            before = pl.multiple_of(first - n, half)
            blocks = [(per_pair(kn_ref, pl.ds(first, n)), per_pair(vn_ref, pl.ds(first, n)), in_reach, False),
                      (per_pair(kn_ref, pl.ds(before, n)), per_pair(vn_ref, pl.ds(before, n)), None, False)]
            chains.append((stacked_q(slice(u * half, (u + 1) * half)), blocks,
                           [jnp.zeros((2 * half, n), F32)] * pairs, [jnp.zeros((half, LANES), F32)] * pairs))
        for u, (c, acc) in enumerate(_sb_blocks(chains, tri_ones, head_lanes)):
            for p in range(pairs):
                for head in range(2):
                    c_s[p, head * qb + u * half:head * qb + (u + 1) * half, :] = c[p][head * half:(head + 1) * half]
                acc_s[p, u * half:(u + 1) * half, :] = acc[p]

    if n_cache:
        run([first_block(), next_block(1), next_block(2)], True)
        limit_rows = jnp.full((2 * qb, 1), n_cache - 2 * n, jnp.int32)
        unvisited = jnp.int32(n_cache - 2 * n)
    else:
        pl.when(i >= 2)(two_halves)
        pl.when(i < 2)(lambda: run([first_block()], True))
        second_half = (row_t >= half).astype(jnp.int32)
        limit_rows = jnp.where(i >= 2, i * qb + (second_half + 1) * half - 2 * n, i * qb)
        unvisited = jnp.where(i >= 2, i * qb + 2 * half - 2 * n, i * qb)

    def live(rem):
        return jnp.logical_and(rem > 0, jnp.max(c_s[...]) >= SB_LOG_ZERO)

    def one_block(rem):
        start = pl.multiple_of(rem - n, n)
        run([older_block(start, start + key_s < limit_rows)], False)
        return start

    lax.while_loop(live, one_block, unvisited)
    o_ref[0] = jnp.concatenate([acc_s[p] for p in range(pairs)], axis=1).astype(o_ref.dtype)


def _sb_attention(q, k_new, v_new, k_cache, v_cache, layer):
    b, t, _ = q.shape
    n_cache = 0 if k_cache is None else k_cache.shape[3]
    qb = min(SB_QUERY_BLOCK, t)
    assert t % qb == 0 and (n_cache == 0 or t == qb)
    n = SB_KEY_BLOCK
    pairs = SB_WIDTH // LANES
    blk = lambda bi, i: (bi, i, 0)
    full = lambda bi, i: (bi, 0, 0)
    in_specs = [pl.BlockSpec((1, qb, SB_WIDTH), blk),
                pl.BlockSpec((1, t, SB_WIDTH), full),
                pl.BlockSpec((1, t, SB_WIDTH), full)]
    args = [q, k_new, v_new]
    scratch = [pltpu.VMEM((pairs, qb, LANES), F32), pltpu.VMEM((pairs, 2 * qb, n), F32)]
    if n_cache:
        newest = lambda bi, i: (layer, bi, 0, n_cache // (2 * n) - 1)
        in_specs += [pl.BlockSpec((None, None, SB_WIDTH, 2 * n), newest)] * 2 + [pl.BlockSpec(memory_space=pl.ANY)] * 2
        args += [k_cache, v_cache, k_cache, v_cache]
        scratch += [pltpu.VMEM((SB_WIDTH, n), F32), pltpu.VMEM((SB_WIDTH, n), F32), pltpu.SemaphoreType.DMA((2,))]
    return pl.pallas_call(
        functools.partial(_sb_kernel, qb=qb, n_cache=n_cache, layer=layer),
        grid=(b, t // qb),
        in_specs=in_specs,
        out_specs=pl.BlockSpec((1, qb, SB_WIDTH), blk),
        out_shape=jax.ShapeDtypeStruct((b, t, SB_WIDTH), BF16),
        scratch_shapes=scratch,
        compiler_params=_params("parallel", "arbitrary"),
        name="sb_attention",
    )(*args)


def _gla_chunk(q, k, v, g, st, masks):
    tril, causal, k_heads, v_heads, st_heads = masks
    g_hi = g.astype(BF16)
    g_lo = (g - g_hi.astype(F32)).astype(BF16)
    b = _dot(tril, g_hi) + _dot(tril, g_lo)
    mid = b[GLA_CHUNK // 2 - 1:GLA_CHUNK // 2, :]
    last = b[GLA_CHUNK - 1:GLA_CHUNK, :]
    q_in = (q * jnp.exp(b - mid)).astype(BF16)
    k_in = k * jnp.exp(mid - b)
    k_rows = (jnp.concatenate([k_in] * GLA_HEADS, axis=0) * k_heads).astype(BF16)
    v_bf = v.astype(BF16)
    v_rows = jnp.concatenate([v_bf] * GLA_HEADS, axis=0) * v_heads
    scores = jnp.where(causal, _dot_nt(q_in, k_rows), 0.0).astype(BF16)
    intra = _dot(scores, v_rows)
    inter = _dot_nt((q * jnp.exp(b)).astype(BF16), st.astype(BF16))
    k_out = (k * jnp.exp(last - b)).astype(BF16)
    st_new = st * jnp.exp(last) + _dot_tn(v_bf, k_out) * st_heads
    return inter + intra, st_new


def _mixer_kernel(x_ref, ya_ref, st0_ref, hist0_ref, norm_ref, wp_ref, wgate_ref, glawg_ref, glabg_ref, glanorm_ref,
                  poolw_ref, poolscale_ref, wa_ref, wb_ref, wc_ref, wout_ref,
                  o_ref, st_ref, hist_ref, st_s, ext_s, *, seqs, rows, n_past):
    t = pl.program_id(1)
    m = seqs * rows

    @pl.when(t == 0)
    def _():
        st_s[...] = st0_ref[...]
        ext_s[:, 0:POOL_HIST_ROWS, :] = hist0_ref[...]

    x = x_ref[...].reshape(m, D_MODEL)
    h = _rms(x, norm_ref[...]).astype(BF16)
    u = _dot(h, wp_ref[...])
    kw, vw = GLA_KEY_WIDTH, GLA_VAL_WIDTH
    q_b = u[:, 0:kw] * GLA_KEY_DIM ** -0.5
    k_b = u[:, kw:2 * kw]
    v_b = u[:, 2 * kw:2 * kw + vw]
    r_b = u[:, 2 * kw + vw:3 * kw + vw]
    o_b = u[:, 3 * kw + vw:3 * kw + 2 * vw]
    u_c = u[:, 3 * kw + 2 * vw:]

    gate = _dot(r_b.astype(BF16), glawg_ref[...]) + glabg_ref[...]
    log_alpha = (jnp.minimum(gate, 0.0) - jnp.log1p(jnp.exp(-jnp.abs(gate)))) / GLA_GATE_TAU
    c = GLA_CHUNK
    ri = lax.broadcasted_iota(jnp.int32, (c, c), 0)
    ci = lax.broadcasted_iota(jnp.int32, (c, c), 1)
    tril = (ci <= ri).astype(BF16)
    key_head = _iota_div((1, kw), 1, GLA_KEY_DIM)
    val_head = _iota_div((1, vw), 1, GLA_VAL_DIM)
    rows_hs = _iota_div((GLA_HEADS * c, 1), 0, c)
    causal = (lax.broadcasted_iota(jnp.int32, (c, GLA_HEADS * c), 1) & (c - 1)) <= lax.broadcasted_iota(
        jnp.int32, (c, GLA_HEADS * c), 0)
    k_heads = (rows_hs == key_head).astype(F32)
    v_heads = (rows_hs == val_head).astype(BF16)
    st_heads = (_iota_div((vw, 1), 0, GLA_VAL_DIM) == key_head).astype(F32)
    masks = (tril, causal, k_heads, v_heads, st_heads)
    o_parts = []
    for s in range(seqs):
        st = st_s[s]
        for j in range(rows // c):
            r0 = s * rows + j * c
            o, st = _gla_chunk(q_b[r0:r0 + c], k_b[r0:r0 + c], v_b[r0:r0 + c], log_alpha[r0:r0 + c], st, masks)
            o_parts.append(o)
        st_s[s] = st
    o = jnp.concatenate(o_parts, axis=0)
    head_mean = (_iota_div((vw, 1), 0, GLA_VAL_DIM) == val_head).astype(BF16) * (1.0 / GLA_VAL_DIM)
    ms = _dot_split(o * o, head_mean)
    y_b = o * lax.rsqrt(ms + RMS_EPS) * glanorm_ref[...] * (o_b * jax.nn.sigmoid(o_b))

    lane = lax.broadcasted_iota(jnp.int32, (1, POOL_WIDTH), 1)
    pos1 = n_past + t * rows + 1 + lax.broadcasted_iota(jnp.int32, (rows, 1), 0)
    d_parts = []
    for s in range(seqs):
        ext_s[s, POOL_HIST_ROWS:, :] = u_c[s * rows:(s + 1) * rows]
        e = ext_s[s]
        sums = []
        span = 1
        for w in POOL_WINDOWS:
            while span < w:
                e = e + pltpu.roll(e, span, 0)
                span *= 2
            sums.append(e[POOL_HIST_ROWS:])
        wsum = sums[-1]
        cnt = jnp.minimum(POOL_WINDOWS[-1], pos1)
        for gi in range(len(POOL_WINDOWS) - 2, -1, -1):
            in_group = lane < (gi + 1) * POOL_GROUP_DIM
            wsum = jnp.where(in_group, sums[gi], wsum)
            cnt = jnp.where(in_group, jnp.minimum(POOL_WINDOWS[gi], pos1), cnt)
        d_parts.append(wsum / cnt.astype(F32) - u_c[s * rows:(s + 1) * rows])
        ext_s[s, 0:POOL_HIST_ROWS, :] = ext_s[s, rows:rows + POOL_HIST_ROWS, :]
    d = jnp.concatenate(d_parts, axis=0)
    y_c = _dot(d.astype(BF16), poolw_ref[...]) * poolscale_ref[...]

    branches = ((ya_ref[...].reshape(m, SB_WIDTH), wa_ref), (y_b.astype(BF16), wb_ref), (y_c.astype(BF16), wc_ref))
    merged = None
    for bi, (y, w_ref) in enumerate(branches):
        gate_b = jax.nn.sigmoid(_dot(h, wgate_ref[:, bi * D_MODEL:(bi + 1) * D_MODEL]))
        term = gate_b * _dot(y, w_ref[...])
        merged = term if merged is None else merged + term
    o_ref[...] = (x + _dot(merged.astype(BF16), wout_ref[...])).reshape(seqs, rows, D_MODEL)

    @pl.when(t == pl.num_programs(1) - 1)
    def _():
        st_ref[...] = st_s[...]
        hist_ref[...] = ext_s[:, 0:POOL_HIST_ROWS, :]


def _mixer(x, y_a, st0, hist0, w, n_past):
    b, t, _ = x.shape
    rows = min(TOKEN_TILE, t)
    seqs = TOKEN_TILE // rows
    assert t % rows == 0 and b % seqs == 0 and rows % GLA_CHUNK == 0
    tok = lambda bi, ti: (bi, ti, 0)
    per_seq = lambda bi, ti: (bi, 0, 0)
    kw, vw = GLA_KEY_WIDTH, GLA_VAL_WIDTH
    return pl.pallas_call(
        functools.partial(_mixer_kernel, seqs=seqs, rows=rows, n_past=n_past),
        grid=(b // seqs, t // rows),
        in_specs=[pl.BlockSpec((seqs, rows, D_MODEL), tok),
                  pl.BlockSpec((seqs, rows, SB_WIDTH), tok),
                  pl.BlockSpec((seqs, vw, kw), per_seq),
                  pl.BlockSpec((seqs, POOL_HIST_ROWS, POOL_WIDTH), per_seq),
                  _const_spec((1, D_MODEL)),
                  _const_spec((D_MODEL, 3 * kw + 3 * vw)),
                  _const_spec((D_MODEL, 3 * D_MODEL)),
                  _const_spec((kw, kw)),
                  _const_spec((1, kw)),
                  _const_spec((1, vw)),
                  _const_spec((POOL_WIDTH, POOL_WIDTH)),
                  _const_spec((1, POOL_WIDTH)),
                  _const_spec((SB_WIDTH, D_MODEL)),
                  _const_spec((vw, D_MODEL)),
                  _const_spec((POOL_WIDTH, D_MODEL)),
                  _const_spec((D_MODEL, D_MODEL))],
        out_specs=[pl.BlockSpec((seqs, rows, D_MODEL), tok),
                   pl.BlockSpec((seqs, vw, kw), per_seq),
                   pl.BlockSpec((seqs, POOL_HIST_ROWS, POOL_WIDTH), per_seq)],
        out_shape=[jax.ShapeDtypeStruct((b, t, D_MODEL), F32),
                   jax.ShapeDtypeStruct((b, vw, kw), F32),
                   jax.ShapeDtypeStruct((b, POOL_HIST_ROWS, POOL_WIDTH), F32)],
        scratch_shapes=[pltpu.VMEM((seqs, vw, kw), F32),
                        pltpu.VMEM((seqs, POOL_HIST_ROWS + rows, POOL_WIDTH), F32)],
        compiler_params=_params("parallel", "arbitrary"),
        name="mixer",
    )(x, y_a, st0, hist0, w["mix_norm"], w["w_proj"], w["w_gate"], w["gla_wg"], w["gla_bg"], w["gla_norm"],
      w["pool_w"], w["pool_scale"], w["w_branch_a"], w["w_branch_b"], w["w_branch_c"], w["w_out"])


def _layer_weights(i, p):
    row = lambda a: a[i].reshape(1, -1).astype(F32)
    bf = lambda a: a.astype(BF16)
    kw, vw = GLA_KEY_WIDTH, GLA_VAL_WIDTH
    w_in = p["w_in"][i]
    a_end = 3 * SB_WIDTH
    qkv_b = w_in[:, a_end:a_end + 2 * kw + vw]
    r0 = a_end + 2 * kw + vw
    r_b = jnp.pad(w_in[:, r0:r0 + GLA_GATE_RANK], ((0, 0), (0, kw - GLA_GATE_RANK)))
    o0 = r0 + GLA_GATE_RANK
    rest = w_in[:, o0:o0 + vw + POOL_WIDTH]
    g0 = o0 + vw + POOL_WIDTH
    pool_w = jnp.zeros((POOL_WIDTH, POOL_WIDTH), F32)
    for gi in range(len(POOL_WINDOWS)):
        lo = gi * POOL_GROUP_DIM
        pool_w = pool_w.at[lo:lo + POOL_GROUP_DIM, lo:lo + POOL_GROUP_DIM].set(p["pool_w"][i, gi])
    return dict(
        ffn1_norm=row(p["ffn1_norm"]), ffn1_wa=bf(p["ffn1_w_in"][i, :, :FFN_DIM]), ffn1_wb=bf(p["ffn1_w_in"][i, :, FFN_DIM:]),
        ffn1_wo=bf(p["ffn1_w_out"][i]),
        mix_norm=row(p["mix_norm"]), w_qkv=bf(w_in[:, :a_end]),
        w_proj=bf(jnp.concatenate([qkv_b, r_b, rest], axis=1)), w_gate=bf(w_in[:, g0:]),
        gla_wg=bf(jnp.pad(p["gla_w_gate"][i], ((0, kw - GLA_GATE_RANK), (0, 0)))),
        gla_bg=row(p["gla_b_gate"]), gla_norm=row(p["gla_norm"]),
        pool_w=bf(pool_w), pool_scale=row(p["pool_scale"]),
        w_branch_a=bf(p["w_branch_a"][i]), w_branch_b=bf(p["w_branch_b"][i]), w_branch_c=bf(p["w_branch_c"][i]),
        w_out=bf(p["w_out"][i]),
        ffn2_norm=row(p["ffn2_norm"]), ffn2_wa=bf(p["ffn2_w_in"][i, :, :FFN_DIM]), ffn2_wb=bf(p["ffn2_w_in"][i, :, FFN_DIM:]),
        ffn2_wo=bf(p["ffn2_w_out"][i]),
        ple_norm=row(p["ple_norm"]), ple_wg=bf(p["ple_w_gate"][i]), ple_wp=bf(p["ple_w_proj"][i]),
    )


def _state_to_blocks(s):
    b = s.shape[0]
    eye = jnp.eye(GLA_HEADS, dtype=s.dtype)
    blocks = jnp.einsum("bhde,hg->bhegd", s, eye)
    return blocks.reshape(b, GLA_VAL_WIDTH, GLA_KEY_WIDTH)


def _blocks_to_state(st):
    b = st.shape[0]
    blocks = st.reshape(b, GLA_HEADS, GLA_VAL_DIM, GLA_HEADS, GLA_KEY_DIM)
    diag = jnp.stack([blocks[:, h, :, h, :] for h in range(GLA_HEADS)], axis=1)
    return diag.transpose(0, 1, 3, 2)


def _run_group(x, p, k_cache, v_cache, s_gla, s_pool, layers, final_norm):
    b, t, _ = x.shape
    n = b * t
    depth = len(layers)
    has_cache = k_cache is not None
    n_past = k_cache.shape[2] if has_cache else 0
    if has_cache:
        k_cache, v_cache = (a.transpose(0, 1, 3, 4, 2).reshape(depth, b, SB_WIDTH, n_past) for a in (k_cache, v_cache))
    x = x.reshape(n, D_MODEL)
    k_all = jnp.zeros((depth, n, SB_WIDTH), F32)
    v_all = jnp.zeros((depth, n, SB_WIDTH), F32)
    ss, ps = [], []
    for i, w in enumerate(layers):
        x = _ffn(x, w)
        q, k, v, k_all, v_all = _qkv(x, w, k_all, v_all, i)
        q, k3, v3 = (a.reshape(b, t, SB_WIDTH) for a in (q, k, v))
        if has_cache:
            st0 = _state_to_blocks(s_gla[i])
            hist0 = jnp.pad(s_pool[i], ((0, 0), (POOL_HIST_ROWS - POOL_HIST, 0), (0, 0)))
        else:
            st0 = jnp.zeros((b, GLA_VAL_WIDTH, GLA_KEY_WIDTH), F32)
            hist0 = jnp.zeros((b, POOL_HIST_ROWS, POOL_WIDTH), F32)
        y_a = _sb_attention(q, k3, v3, k_cache, v_cache, i)
        x3, st, hist = _mixer(x.reshape(b, t, D_MODEL), y_a, st0, hist0, w, n_past)
        x = _ffn_ple(x3.reshape(n, D_MODEL), p.reshape(depth, n, PLE_DIM), i, w, final_norm, final=(i == depth - 1))
        ss.append(_blocks_to_state(st))
        ps.append(hist[:, POOL_HIST_ROWS - POOL_HIST:, :])
    heads = (depth, b, t, SB_HEADS, SB_HEAD_DIM)
    return x.reshape(b, t, D_MODEL), k_all.reshape(heads), v_all.reshape(heads), jnp.stack(ss), jnp.stack(ps)


def kernel(x_prompt, x_sample, cache_sb_k, cache_sb_v, state_gla, state_pool, p_prompt, p_sample, ffn1_norm, ffn1_w_in, ffn1_w_out, mix_norm, w_in, gla_w_gate, gla_b_gate, gla_norm, pool_w, pool_scale, w_branch_a, w_branch_b, w_branch_c, w_out, ffn2_norm, ffn2_w_in, ffn2_w_out, ple_norm, ple_w_gate, ple_w_proj, final_norm):
    params = dict(ffn1_norm=ffn1_norm, ffn1_w_in=ffn1_w_in, ffn1_w_out=ffn1_w_out, mix_norm=mix_norm, w_in=w_in,
                  gla_w_gate=gla_w_gate, gla_b_gate=gla_b_gate, gla_norm=gla_norm, pool_w=pool_w, pool_scale=pool_scale,
                  w_branch_a=w_branch_a, w_branch_b=w_branch_b, w_branch_c=w_branch_c, w_out=w_out,
                  ffn2_norm=ffn2_norm, ffn2_w_in=ffn2_w_in, ffn2_w_out=ffn2_w_out, ple_norm=ple_norm,
                  ple_w_gate=ple_w_gate, ple_w_proj=ple_w_proj)
    depth = w_in.shape[0]
    layers = [_layer_weights(i, params) for i in range(depth)]
    fnorm = final_norm.reshape(1, D_MODEL).astype(F32)
    prompt = _run_group(x_prompt, p_prompt, None, None, None, None, layers, fnorm)
    sample = _run_group(x_sample, p_sample, cache_sb_k, cache_sb_v, state_gla, state_pool, layers, fnorm)
    return (prompt[0], sample[0], prompt[1], prompt[2], prompt[3], prompt[4],
            sample[1], sample[2], sample[3], sample[4])
```

```python
import functools

import jax
import jax.numpy as jnp
from jax import lax
from jax.experimental import pallas as pl
from jax.experimental.pallas import tpu as pltpu

D_MODEL = 1024
FFN_DIM = 2816
PLE_DIM = 256
SB_HEADS = 8
SB_HEAD_DIM = 64
SB_WIDTH = SB_HEADS * SB_HEAD_DIM
GLA_HEADS = 4
GLA_KEY_DIM = 32
GLA_VAL_DIM = 64
GLA_KEY_WIDTH = GLA_HEADS * GLA_KEY_DIM
GLA_VAL_WIDTH = GLA_HEADS * GLA_VAL_DIM
GLA_GATE_RANK = 16
GLA_GATE_TAU = 16.0
GLA_CHUNK = 64
POOL_WINDOWS = (2, 4, 8, 16)
POOL_GROUP_DIM = 64
POOL_WIDTH = len(POOL_WINDOWS) * POOL_GROUP_DIM
POOL_HIST = 15
POOL_HIST_ROWS = 16
RMS_EPS = 1e-6

LANES = 128
VMEM_LIMIT_BYTES = 56 * 1024 * 1024
TOKEN_TILE = 512
FFN_CHUNK = 256
SB_QUERY_BLOCK = 128
SB_KEY_BLOCK = 128
SB_LOG_ZERO = -104.0

F32 = jnp.float32
BF16 = jnp.bfloat16


def _dot(a, b):
    return jnp.dot(a, b, preferred_element_type=F32)


def _dot_nt(a, b):
    return lax.dot_general(a, b, (((1,), (1,)), ((), ())), preferred_element_type=F32)


def _dot_tn(a, b):
    return lax.dot_general(a, b, (((0,), (0,)), ((), ())), preferred_element_type=F32)


def _dot_split(a, b):
    hi = a.astype(BF16)
    lo = (a - hi.astype(F32)).astype(BF16)
    return _dot(hi, b) + _dot(lo, b)


def _iota_div(shape, axis, size):
    assert size & (size - 1) == 0
    return lax.broadcasted_iota(jnp.int32, shape, axis) >> (size.bit_length() - 1)


def _rms(x, g):
    ms = jnp.mean(x * x, axis=-1, keepdims=True)
    return x * lax.rsqrt(ms + RMS_EPS) * g


def _params(*semantics):
    return pltpu.CompilerParams(dimension_semantics=semantics, vmem_limit_bytes=VMEM_LIMIT_BYTES)


def _const_spec(shape):
    zeros = (0,) * len(shape)
    return pl.BlockSpec(shape, lambda *_: zeros, pipeline_mode=pl.Buffered(1))


def _swiglu_update(x, norm_ref, wa_ref, wb_ref, wo_ref):
    h = _rms(x, norm_ref[...]).astype(BF16)
    acc = jnp.zeros(x.shape, F32)
    for f in range(0, FFN_DIM, FFN_CHUNK):
        a = _dot(h, wa_ref[:, f:f + FFN_CHUNK])
        b = _dot(h, wb_ref[:, f:f + FFN_CHUNK])
        g = (a * jax.nn.sigmoid(a) * b).astype(BF16)
        acc = acc + _dot(g, wo_ref[f:f + FFN_CHUNK, :])
    return x + 0.5 * acc


def _ffn_kernel(x_ref, norm_ref, wa_ref, wb_ref, wo_ref, o_ref):
    o_ref[...] = _swiglu_update(x_ref[...], norm_ref, wa_ref, wb_ref, wo_ref)


def _ffn_ple_kernel(x_ref, p_ref, norm_ref, wa_ref, wb_ref, wo_ref, pnorm_ref, wpg_ref, wpp_ref, fnorm_ref,
                    o_ref, *, final):
    x = _swiglu_update(x_ref[...], norm_ref, wa_ref, wb_ref, wo_ref)
    gate = jax.nn.sigmoid(_dot(_rms(x, pnorm_ref[...]).astype(BF16), wpg_ref[...]))
    x = x + gate * _dot(p_ref[...].astype(BF16), wpp_ref[...])
    if final:
        x = _rms(x, fnorm_ref[...])
    o_ref[...] = x


def _ffn(x, w):
    n = x.shape[0]
    tok = lambda i: (i, 0)
    return pl.pallas_call(
        _ffn_kernel,
        grid=(n // TOKEN_TILE,),
        in_specs=[pl.BlockSpec((TOKEN_TILE, D_MODEL), tok),
                  _const_spec((1, D_MODEL)),
                  _const_spec((D_MODEL, FFN_DIM)),
                  _const_spec((D_MODEL, FFN_DIM)),
                  _const_spec((FFN_DIM, D_MODEL))],
        out_specs=pl.BlockSpec((TOKEN_TILE, D_MODEL), tok),
        out_shape=jax.ShapeDtypeStruct((n, D_MODEL), F32),
        compiler_params=_params("parallel"),
        name="ffn",
    )(x, w["ffn1_norm"], w["ffn1_wa"], w["ffn1_wb"], w["ffn1_wo"])


def _ffn_ple(x, p, layer, w, final_norm, final):
    n = x.shape[0]
    tok = lambda i: (i, 0)
    return pl.pallas_call(
        functools.partial(_ffn_ple_kernel, final=final),
        grid=(n // TOKEN_TILE,),
        in_specs=[pl.BlockSpec((TOKEN_TILE, D_MODEL), tok),
                  pl.BlockSpec((None, TOKEN_TILE, PLE_DIM), lambda i: (layer, i, 0)),
                  _const_spec((1, D_MODEL)),
                  _const_spec((D_MODEL, FFN_DIM)),
                  _const_spec((D_MODEL, FFN_DIM)),
                  _const_spec((FFN_DIM, D_MODEL)),
                  _const_spec((1, D_MODEL)),
                  _const_spec((D_MODEL, D_MODEL)),
                  _const_spec((PLE_DIM, D_MODEL)),
                  _const_spec((1, D_MODEL))],
        out_specs=pl.BlockSpec((TOKEN_TILE, D_MODEL), tok),
        out_shape=jax.ShapeDtypeStruct((n, D_MODEL), F32),
        compiler_params=_params("parallel"),
        name="ffn_ple",
    )(x, p, w["ffn2_norm"], w["ffn2_wa"], w["ffn2_wb"], w["ffn2_wo"], w["ple_norm"], w["ple_wg"], w["ple_wp"],
      final_norm)


def _qkv_kernel(x_ref, norm_ref, w_ref, k_all_ref, v_all_ref, q_ref, kb_ref, vb_ref, k_ref, v_ref):
    del k_all_ref, v_all_ref
    h = _rms(x_ref[...], norm_ref[...]).astype(BF16)
    u = _dot(h, w_ref[...])
    q_ref[...] = (u[:, :SB_WIDTH] * SB_HEAD_DIM ** -0.5).astype(BF16)
    k = u[:, SB_WIDTH:2 * SB_WIDTH]
    v = u[:, 2 * SB_WIDTH:]
    for head in range(SB_HEADS):
        cols = slice(head * SB_HEAD_DIM, (head + 1) * SB_HEAD_DIM)
        k_ref[pl.ds(head, TOKEN_TILE, stride=SB_HEADS), :] = k[:, cols]
        v_ref[pl.ds(head, TOKEN_TILE, stride=SB_HEADS), :] = v[:, cols]
    kb_ref[...] = k.astype(BF16)
    vb_ref[...] = v.astype(BF16)


def _qkv(x, w, k_all, v_all, layer):
    n = x.shape[0]
    tok = lambda i: (i, 0)
    layer_tok = lambda i: (layer, i, 0)
    return pl.pallas_call(
        _qkv_kernel,
        grid=(n // TOKEN_TILE,),
        in_specs=[pl.BlockSpec((TOKEN_TILE, D_MODEL), tok),
                  _const_spec((1, D_MODEL)),
                  _const_spec((D_MODEL, 3 * SB_WIDTH)),
                  pl.BlockSpec(memory_space=pl.ANY),
                  pl.BlockSpec(memory_space=pl.ANY)],
        out_specs=[pl.BlockSpec((TOKEN_TILE, SB_WIDTH), tok)] * 3
        + [pl.BlockSpec((None, TOKEN_TILE * SB_HEADS, SB_HEAD_DIM), layer_tok)] * 2,
        out_shape=[jax.ShapeDtypeStruct((n, SB_WIDTH), BF16)] * 3
        + [jax.ShapeDtypeStruct(k_all.shape, F32), jax.ShapeDtypeStruct(v_all.shape, F32)],
        input_output_aliases={3: 3, 4: 4},
        compiler_params=_params("parallel"),
        name="qkv",
    )(x, w["mix_norm"], w["w_qkv"], k_all, v_all)


def _sb_blocks(chains, tri_ones, head_lanes):
    n = tri_ones.shape[0] // 2
    row = lax.broadcasted_iota(jnp.int32, (LANES, 1), 0)
    head_rows = (row < SB_HEAD_DIM, row >= SB_HEAD_DIM)
    log_beta, split = [], []
    for q2, blocks, _, _ in chains:
        for k, _, mask, transposed in blocks:
            for p in range(len(q2)):
                z = _dot(q2[p], k[p]) if transposed else _dot_nt(q2[p], k[p])
                t = jnp.log(1.0 + jnp.exp(-jnp.abs(z)))
                lk = -(jnp.maximum(z, 0.0) + t)
                if mask is not None:
                    lk = jnp.where(mask, lk, 0.0)
                hi = lk.astype(BF16)
                lo = (lk - hi.astype(F32)).astype(BF16)
                log_beta.append(jnp.minimum(z, 0.0) - t)
                split.append(jnp.concatenate([hi, lo], axis=1))
    x = _dot(jnp.concatenate(split, axis=0), tri_ones)
    results = []
    visit = 0
    r0 = 0
    for q2, blocks, c, acc in chains:
        c, acc = list(c), list(acc)
        rows = acc[0].shape[0]
        for _, v, mask, transposed in blocks:
            for p in range(len(q2)):
                xs = x[r0:r0 + 2 * rows]
                w = jnp.exp(log_beta[visit] + xs[:, :n] + c[p])
                if mask is not None:
                    w = jnp.where(mask, w, 0.0)
                c[p] = c[p] + xs[:, n:]
                w2 = jnp.concatenate([w[:rows], w[rows:]], axis=1).astype(BF16)
                zero = jnp.zeros_like(v[p])
                if transposed:
                    v2 = jnp.concatenate([jnp.where(m, v[p], zero) for m in head_rows], axis=1)
                    acc[p] = acc[p] + _dot_nt(w2, v2)
                else:
                    v2 = jnp.concatenate([jnp.where(m, v[p], zero) for m in head_lanes], axis=0)
                    acc[p] = acc[p] + _dot(w2, v2)
                visit += 1
                r0 += 2 * rows
        results.append((c, acc))
    return results


def _sb_kernel(*refs, qb, n_cache, layer):
    n = SB_KEY_BLOCK
    pairs = SB_WIDTH // LANES
    if n_cache:
        (q_ref, kn_ref, vn_ref, kwin_ref, vwin_ref, kc_hbm, vc_hbm, o_ref, acc_s, c_s, kold_s, vold_s,
         old_sem) = refs
        assert n_cache % n == 0 and n_cache >= 2 * n and kn_ref.shape[1] == qb
    else:
        q_ref, kn_ref, vn_ref, o_ref, acc_s, c_s = refs
        assert qb == n
    bi = pl.program_id(0)
    i = pl.program_id(1)

    lane = lax.broadcasted_iota(jnp.int32, (1, LANES), 1)
    head_lanes = (lane < SB_HEAD_DIM, lane >= SB_HEAD_DIM)
    tri_row = lax.broadcasted_iota(jnp.int32, (2 * n, 2 * n), 0) & (n - 1)
    tri_col = lax.broadcasted_iota(jnp.int32, (2 * n, 2 * n), 1)
    tri_ones = jnp.logical_or(tri_col >= n, tri_row > tri_col).astype(BF16)
    key_s = lax.broadcasted_iota(jnp.int32, (1, n), 1)
    row_t = lax.broadcasted_iota(jnp.int32, (2 * qb, 1), 0) & (qb - 1)
    causal = key_s < row_t

    def lanes(p):
        return slice(p * LANES, (p + 1) * LANES)

    def per_pair(ref, rows):
        return [ref[0, rows, lanes(p)].astype(BF16) for p in range(pairs)]

    def per_pair_t(ref, cols):
        return [ref[lanes(p), cols].astype(BF16) for p in range(pairs)]

    def stacked_q(rows=slice(None)):
        q2 = []
        for p in range(pairs):
            qp = q_ref[0, rows, lanes(p)]
            q2.append(jnp.concatenate([jnp.where(m, qp, jnp.zeros_like(qp)) for m in head_lanes], axis=0))
        return q2

    def first_block():
        own = pl.ds(pl.multiple_of(i * qb, qb), qb)
        ks, vs = per_pair(kn_ref, own), per_pair(vn_ref, own)
        if qb < n:
            pad = jnp.zeros((n - qb, LANES), BF16)
            ks = [jnp.concatenate([x, pad], axis=0) for x in ks]
            vs = [jnp.concatenate([x, pad], axis=0) for x in vs]
        return ks, vs, causal, False

    def next_block(d):
        if n_cache:
            cols = slice((2 - d) * n, (3 - d) * n)
            return per_pair_t(kwin_ref, cols), per_pair_t(vwin_ref, cols), None, True
        rows = pl.ds(pl.multiple_of((i - d) * n, n), n)
        return per_pair(kn_ref, rows), per_pair(vn_ref, rows), None, False

    def older_block(start, mask):
        if n_cache:
            cols = pl.ds(start, n)
            copies = [pltpu.make_async_copy(kc_hbm.at[layer, bi, :, cols], kold_s, old_sem.at[0]),
                      pltpu.make_async_copy(vc_hbm.at[layer, bi, :, cols], vold_s, old_sem.at[1])]
            for cp in copies:
                cp.start()
            for cp in copies:
                cp.wait()
            return per_pair_t(kold_s, slice(None)), per_pair_t(vold_s, slice(None)), mask, True
        rows = pl.ds(start, n)
        return per_pair(kn_ref, rows), per_pair(vn_ref, rows), mask, False

    def run(blocks, fresh_start):
        if fresh_start:
            c = [jnp.zeros((2 * qb, n), F32)] * pairs
            acc = [jnp.zeros((qb, LANES), F32)] * pairs
        else:
            c = [c_s[p] for p in range(pairs)]
            acc = [acc_s[p] for p in range(pairs)]
        ((c, acc),) = _sb_blocks([(stacked_q(), blocks, c, acc)], tri_ones, head_lanes)
        for p in range(pairs):
            c_s[p], acc_s[p] = c[p], acc[p]

    half = qb // 2

    def two_halves():
        in_reach = key_s < (row_t & (half - 1))[:2 * half] + (n - half)
        chains = []
        for u in range(2):
            first = pl.multiple_of(i * qb + (u + 1) * half - n, half)
            before = pl.multiple_of(first - n, half)
            blocks = [(per_pair(kn_ref, pl.ds(first, n)), per_pair(vn_ref, pl.ds(first, n)), in_reach, False),
                      (per_pair(kn_ref, pl.ds(before, n)), per_pair(vn_ref, pl.ds(before, n)), None, False)]
            chains.append((stacked_q(slice(u * half, (u + 1) * half)), blocks,
                           [jnp.zeros((2 * half, n), F32)] * pairs, [jnp.zeros((half, LANES), F32)] * pairs))
        for u, (c, acc) in enumerate(_sb_blocks(chains, tri_ones, head_lanes)):
            for p in range(pairs):
                for head in range(2):
                    c_s[p, head * qb + u * half:head * qb + (u + 1) * half, :] = c[p][head * half:(head + 1) * half]
                acc_s[p, u * half:(u + 1) * half, :] = acc[p]

    if n_cache:
        run([first_block(), next_block(1), next_block(2)], True)
        limit_rows = jnp.full((2 * qb, 1), n_cache - 2 * n, jnp.int32)
        unvisited = jnp.int32(n_cache - 2 * n)
    else:
        pl.when(i >= 2)(two_halves)
        pl.when(i < 2)(lambda: run([first_block()], True))
        second_half = (row_t >= half).astype(jnp.int32)
        limit_rows = jnp.where(i >= 2, i * qb + (second_half + 1) * half - 2 * n, i * qb)
        unvisited = jnp.where(i >= 2, i * qb + 2 * half - 2 * n, i * qb)

    def live(rem):
        return jnp.logical_and(rem > 0, jnp.max(c_s[...]) >= SB_LOG_ZERO)

    def one_block(rem):
        start = pl.multiple_of(rem - n, n)
        run([older_block(start, start + key_s < limit_rows)], False)
        return start

    lax.while_loop(live, one_block, unvisited)
    o_ref[0] = jnp.concatenate([acc_s[p] for p in range(pairs)], axis=1).astype(o_ref.dtype)


def _sb_attention(q, k_new, v_new, k_cache, v_cache, layer):
    b, t, _ = q.shape
    n_cache = 0 if k_cache is None else k_cache.shape[3]
    qb = min(SB_QUERY_BLOCK, t)
    assert t % qb == 0 and (n_cache == 0 or t == qb)
    n = SB_KEY_BLOCK
    pairs = SB_WIDTH // LANES
    blk = lambda bi, i: (bi, i, 0)
    full = lambda bi, i: (bi, 0, 0)
    in_specs = [pl.BlockSpec((1, qb, SB_WIDTH), blk),
                pl.BlockSpec((1, t, SB_WIDTH), full),
                pl.BlockSpec((1, t, SB_WIDTH), full)]
    args = [q, k_new, v_new]
    scratch = [pltpu.VMEM((pairs, qb, LANES), F32), pltpu.VMEM((pairs, 2 * qb, n), F32)]
    if n_cache:
        newest = lambda bi, i: (layer, bi, 0, n_cache // (2 * n) - 1)
        in_specs += [pl.BlockSpec((None, None, SB_WIDTH, 2 * n), newest)] * 2 + [pl.BlockSpec(memory_space=pl.ANY)] * 2
        args += [k_cache, v_cache, k_cache, v_cache]
        scratch += [pltpu.VMEM((SB_WIDTH, n), F32), pltpu.VMEM((SB_WIDTH, n), F32), pltpu.SemaphoreType.DMA((2,))]
    return pl.pallas_call(
        functools.partial(_sb_kernel, qb=qb, n_cache=n_cache, layer=layer),
        grid=(b, t // qb),
        in_specs=in_specs,
        out_specs=pl.BlockSpec((1, qb, SB_WIDTH), blk),
        out_shape=jax.ShapeDtypeStruct((b, t, SB_WIDTH), BF16),
        scratch_shapes=scratch,
        compiler_params=_params("parallel", "arbitrary"),
        name="sb_attention",
    )(*args)


def _gla_chunk(q, k, v, g, st, masks):
    tril, causal, k_heads, v_heads, st_heads = masks
    g_hi = g.astype(BF16)
    g_lo = (g - g_hi.astype(F32)).astype(BF16)
    b = _dot(tril, g_hi) + _dot(tril, g_lo)
    mid = b[GLA_CHUNK // 2 - 1:GLA_CHUNK // 2, :]
    last = b[GLA_CHUNK - 1:GLA_CHUNK, :]
    q_in = (q * jnp.exp(b - mid)).astype(BF16)
    k_in = k * jnp.exp(mid - b)
    k_rows = (jnp.concatenate([k_in] * GLA_HEADS, axis=0) * k_heads).astype(BF16)
    v_bf = v.astype(BF16)
    v_rows = jnp.concatenate([v_bf] * GLA_HEADS, axis=0) * v_heads
    scores = jnp.where(causal, _dot_nt(q_in, k_rows), 0.0).astype(BF16)
    intra = _dot(scores, v_rows)
    inter = _dot_nt((q * jnp.exp(b)).astype(BF16), st.astype(BF16))
    k_out = (k * jnp.exp(last - b)).astype(BF16)
    st_new = st * jnp.exp(last) + _dot_tn(v_bf, k_out) * st_heads
    return inter + intra, st_new


def _mixer_kernel(x_ref, ya_ref, st0_ref, hist0_ref, norm_ref, wp_ref, wgate_ref, glawg_ref, glabg_ref, glanorm_ref,
                  poolw_ref, poolscale_ref, wa_ref, wb_ref, wc_ref, wout_ref,
                  o_ref, st_ref, hist_ref, st_s, ext_s, *, seqs, rows, n_past):
    t = pl.program_id(1)
    m = seqs * rows

    @pl.when(t == 0)
    def _():
        st_s[...] = st0_ref[...]
        ext_s[:, 0:POOL_HIST_ROWS, :] = hist0_ref[...]

    x = x_ref[...].reshape(m, D_MODEL)
    h = _rms(x, norm_ref[...]).astype(BF16)
    u = _dot(h, wp_ref[...])
    kw, vw = GLA_KEY_WIDTH, GLA_VAL_WIDTH
    q_b = u[:, 0:kw] * GLA_KEY_DIM ** -0.5
    k_b = u[:, kw:2 * kw]
    v_b = u[:, 2 * kw:2 * kw + vw]
    r_b = u[:, 2 * kw + vw:3 * kw + vw]
    o_b = u[:, 3 * kw + vw:3 * kw + 2 * vw]
    u_c = u[:, 3 * kw + 2 * vw:]

    gate = _dot(r_b.astype(BF16), glawg_ref[...]) + glabg_ref[...]
    log_alpha = (jnp.minimum(gate, 0.0) - jnp.log1p(jnp.exp(-jnp.abs(gate)))) / GLA_GATE_TAU
    c = GLA_CHUNK
    ri = lax.broadcasted_iota(jnp.int32, (c, c), 0)
    ci = lax.broadcasted_iota(jnp.int32, (c, c), 1)
    tril = (ci <= ri).astype(BF16)
    key_head = _iota_div((1, kw), 1, GLA_KEY_DIM)
    val_head = _iota_div((1, vw), 1, GLA_VAL_DIM)
    rows_hs = _iota_div((GLA_HEADS * c, 1), 0, c)
    causal = (lax.broadcasted_iota(jnp.int32, (c, GLA_HEADS * c), 1) & (c - 1)) <= lax.broadcasted_iota(
        jnp.int32, (c, GLA_HEADS * c), 0)
    k_heads = (rows_hs == key_head).astype(F32)
    v_heads = (rows_hs == val_head).astype(BF16)
    st_heads = (_iota_div((vw, 1), 0, GLA_VAL_DIM) == key_head).astype(F32)
    masks = (tril, causal, k_heads, v_heads, st_heads)
    o_parts = []
    for s in range(seqs):
        st = st_s[s]
        for j in range(rows // c):
            r0 = s * rows + j * c
            o, st = _gla_chunk(q_b[r0:r0 + c], k_b[r0:r0 + c], v_b[r0:r0 + c], log_alpha[r0:r0 + c], st, masks)
            o_parts.append(o)
        st_s[s] = st
    o = jnp.concatenate(o_parts, axis=0)
    head_mean = (_iota_div((vw, 1), 0, GLA_VAL_DIM) == val_head).astype(BF16) * (1.0 / GLA_VAL_DIM)
    ms = _dot_split(o * o, head_mean)
    y_b = o * lax.rsqrt(ms + RMS_EPS) * glanorm_ref[...] * (o_b * jax.nn.sigmoid(o_b))

    lane = lax.broadcasted_iota(jnp.int32, (1, POOL_WIDTH), 1)
    pos1 = n_past + t * rows + 1 + lax.broadcasted_iota(jnp.int32, (rows, 1), 0)
    d_parts = []
    for s in range(seqs):
        ext_s[s, POOL_HIST_ROWS:, :] = u_c[s * rows:(s + 1) * rows]
        e = ext_s[s]
        sums = []
        span = 1
        for w in POOL_WINDOWS:
            while span < w:
                e = e + pltpu.roll(e, span, 0)
                span *= 2
            sums.append(e[POOL_HIST_ROWS:])
        wsum = sums[-1]
        cnt = jnp.minimum(POOL_WINDOWS[-1], pos1)
        for gi in range(len(POOL_WINDOWS) - 2, -1, -1):
            in_group = lane < (gi + 1) * POOL_GROUP_DIM
            wsum = jnp.where(in_group, sums[gi], wsum)
            cnt = jnp.where(in_group, jnp.minimum(POOL_WINDOWS[gi], pos1), cnt)
        d_parts.append(wsum / cnt.astype(F32) - u_c[s * rows:(s + 1) * rows])
        ext_s[s, 0:POOL_HIST_ROWS, :] = ext_s[s, rows:rows + POOL_HIST_ROWS, :]
    d = jnp.concatenate(d_parts, axis=0)
    y_c = _dot(d.astype(BF16), poolw_ref[...]) * poolscale_ref[...]

    branches = ((ya_ref[...].reshape(m, SB_WIDTH), wa_ref), (y_b.astype(BF16), wb_ref), (y_c.astype(BF16), wc_ref))
    merged = None
    for bi, (y, w_ref) in enumerate(branches):
        gate_b = jax.nn.sigmoid(_dot(h, wgate_ref[:, bi * D_MODEL:(bi + 1) * D_MODEL]))
        term = gate_b * _dot(y, w_ref[...])
        merged = term if merged is None else merged + term
    o_ref[...] = (x + _dot(merged.astype(BF16), wout_ref[...])).reshape(seqs, rows, D_MODEL)

    @pl.when(t == pl.num_programs(1) - 1)
    def _():
        st_ref[...] = st_s[...]
        hist_ref[...] = ext_s[:, 0:POOL_HIST_ROWS, :]


def _mixer(x, y_a, st0, hist0, w, n_past):
    b, t, _ = x.shape
    rows = min(TOKEN_TILE, t)
    seqs = TOKEN_TILE // rows
    assert t % rows == 0 and b % seqs == 0 and rows % GLA_CHUNK == 0
    tok = lambda bi, ti: (bi, ti, 0)
    per_seq = lambda bi, ti: (bi, 0, 0)
    kw, vw = GLA_KEY_WIDTH, GLA_VAL_WIDTH
    return pl.pallas_call(
        functools.partial(_mixer_kernel, seqs=seqs, rows=rows, n_past=n_past),
        grid=(b // seqs, t // rows),
        in_specs=[pl.BlockSpec((seqs, rows, D_MODEL), tok),
                  pl.BlockSpec((seqs, rows, SB_WIDTH), tok),
                  pl.BlockSpec((seqs, vw, kw), per_seq),
                  pl.BlockSpec((seqs, POOL_HIST_ROWS, POOL_WIDTH), per_seq),
                  _const_spec((1, D_MODEL)),
                  _const_spec((D_MODEL, 3 * kw + 3 * vw)),
                  _const_spec((D_MODEL, 3 * D_MODEL)),
                  _const_spec((kw, kw)),
                  _const_spec((1, kw)),
                  _const_spec((1, vw)),
                  _const_spec((POOL_WIDTH, POOL_WIDTH)),
                  _const_spec((1, POOL_WIDTH)),
                  _const_spec((SB_WIDTH, D_MODEL)),
                  _const_spec((vw, D_MODEL)),
                  _const_spec((POOL_WIDTH, D_MODEL)),
                  _const_spec((D_MODEL, D_MODEL))],
        out_specs=[pl.BlockSpec((seqs, rows, D_MODEL), tok),
                   pl.BlockSpec((seqs, vw, kw), per_seq),
                   pl.BlockSpec((seqs, POOL_HIST_ROWS, POOL_WIDTH), per_seq)],
        out_shape=[jax.ShapeDtypeStruct((b, t, D_MODEL), F32),
                   jax.ShapeDtypeStruct((b, vw, kw), F32),
                   jax.ShapeDtypeStruct((b, POOL_HIST_ROWS, POOL_WIDTH), F32)],
        scratch_shapes=[pltpu.VMEM((seqs, vw, kw), F32),
                        pltpu.VMEM((seqs, POOL_HIST_ROWS + rows, POOL_WIDTH), F32)],
        compiler_params=_params("parallel", "arbitrary"),
        name="mixer",
    )(x, y_a, st0, hist0, w["mix_norm"], w["w_proj"], w["w_gate"], w["gla_wg"], w["gla_bg"], w["gla_norm"],
      w["pool_w"], w["pool_scale"], w["w_branch_a"], w["w_branch_b"], w["w_branch_c"], w["w_out"])


def _layer_weights(i, p):
    row = lambda a: a[i].reshape(1, -1).astype(F32)
    bf = lambda a: a.astype(BF16)
    kw, vw = GLA_KEY_WIDTH, GLA_VAL_WIDTH
    w_in = p["w_in"][i]
    a_end = 3 * SB_WIDTH
    qkv_b = w_in[:, a_end:a_end + 2 * kw + vw]
    r0 = a_end + 2 * kw + vw
    r_b = jnp.pad(w_in[:, r0:r0 + GLA_GATE_RANK], ((0, 0), (0, kw - GLA_GATE_RANK)))
    o0 = r0 + GLA_GATE_RANK
    rest = w_in[:, o0:o0 + vw + POOL_WIDTH]
    g0 = o0 + vw + POOL_WIDTH
    pool_w = jnp.zeros((POOL_WIDTH, POOL_WIDTH), F32)
    for gi in range(len(POOL_WINDOWS)):
        lo = gi * POOL_GROUP_DIM
        pool_w = pool_w.at[lo:lo + POOL_GROUP_DIM, lo:lo + POOL_GROUP_DIM].set(p["pool_w"][i, gi])
    return dict(
        ffn1_norm=row(p["ffn1_norm"]), ffn1_wa=bf(p["ffn1_w_in"][i, :, :FFN_DIM]), ffn1_wb=bf(p["ffn1_w_in"][i, :, FFN_DIM:]),
        ffn1_wo=bf(p["ffn1_w_out"][i]),
        mix_norm=row(p["mix_norm"]), w_qkv=bf(w_in[:, :a_end]),
        w_proj=bf(jnp.concatenate([qkv_b, r_b, rest], axis=1)), w_gate=bf(w_in[:, g0:]),
        gla_wg=bf(jnp.pad(p["gla_w_gate"][i], ((0, kw - GLA_GATE_RANK), (0, 0)))),
        gla_bg=row(p["gla_b_gate"]), gla_norm=row(p["gla_norm"]),
        pool_w=bf(pool_w), pool_scale=row(p["pool_scale"]),
        w_branch_a=bf(p["w_branch_a"][i]), w_branch_b=bf(p["w_branch_b"][i]), w_branch_c=bf(p["w_branch_c"][i]),
        w_out=bf(p["w_out"][i]),
        ffn2_norm=row(p["ffn2_norm"]), ffn2_wa=bf(p["ffn2_w_in"][i, :, :FFN_DIM]), ffn2_wb=bf(p["ffn2_w_in"][i, :, FFN_DIM:]),
        ffn2_wo=bf(p["ffn2_w_out"][i]),
        ple_norm=row(p["ple_norm"]), ple_wg=bf(p["ple_w_gate"][i]), ple_wp=bf(p["ple_w_proj"][i]),
    )


def _state_to_blocks(s):
    b = s.shape[0]
    eye = jnp.eye(GLA_HEADS, dtype=s.dtype)
    blocks = jnp.einsum("bhde,hg->bhegd", s, eye)
    return blocks.reshape(b, GLA_VAL_WIDTH, GLA_KEY_WIDTH)


def _blocks_to_state(st):
    b = st.shape[0]
    blocks = st.reshape(b, GLA_HEADS, GLA_VAL_DIM, GLA_HEADS, GLA_KEY_DIM)
    diag = jnp.stack([blocks[:, h, :, h, :] for h in range(GLA_HEADS)], axis=1)
    return diag.transpose(0, 1, 3, 2)


def _run_group(x, p, k_cache, v_cache, s_gla, s_pool, layers, final_norm):
    b, t, _ = x.shape
    n = b * t
    depth = len(layers)
    has_cache = k_cache is not None
    n_past = k_cache.shape[2] if has_cache else 0
    if has_cache:
        k_cache, v_cache = (a.transpose(0, 1, 3, 4, 2).reshape(depth, b, SB_WIDTH, n_past) for a in (k_cache, v_cache))
    x = x.reshape(n, D_MODEL)
    k_all = jnp.zeros((depth, n * SB_HEADS, SB_HEAD_DIM), F32)
    v_all = jnp.zeros((depth, n * SB_HEADS, SB_HEAD_DIM), F32)
    ss, ps = [], []
    for i, w in enumerate(layers):
        x = _ffn(x, w)
        q, k, v, k_all, v_all = _qkv(x, w, k_all, v_all, i)
        q, k3, v3 = (a.reshape(b, t, SB_WIDTH) for a in (q, k, v))
        if has_cache:
            st0 = _state_to_blocks(s_gla[i])
            hist0 = jnp.pad(s_pool[i], ((0, 0), (POOL_HIST_ROWS - POOL_HIST, 0), (0, 0)))
        else:
            st0 = jnp.zeros((b, GLA_VAL_WIDTH, GLA_KEY_WIDTH), F32)
            hist0 = jnp.zeros((b, POOL_HIST_ROWS, POOL_WIDTH), F32)
        y_a = _sb_attention(q, k3, v3, k_cache, v_cache, i)
        x3, st, hist = _mixer(x.reshape(b, t, D_MODEL), y_a, st0, hist0, w, n_past)
        x = _ffn_ple(x3.reshape(n, D_MODEL), p.reshape(depth, n, PLE_DIM), i, w, final_norm, final=(i == depth - 1))
        ss.append(_blocks_to_state(st))
        ps.append(hist[:, POOL_HIST_ROWS - POOL_HIST:, :])
    heads = (depth, b, t, SB_HEADS, SB_HEAD_DIM)
    return x.reshape(b, t, D_MODEL), k_all.reshape(heads), v_all.reshape(heads), jnp.stack(ss), jnp.stack(ps)


def kernel(x_prompt, x_sample, cache_sb_k, cache_sb_v, state_gla, state_pool, p_prompt, p_sample, ffn1_norm, ffn1_w_in, ffn1_w_out, mix_norm, w_in, gla_w_gate, gla_b_gate, gla_norm, pool_w, pool_scale, w_branch_a, w_branch_b, w_branch_c, w_out, ffn2_norm, ffn2_w_in, ffn2_w_out, ple_norm, ple_w_gate, ple_w_proj, final_norm):
    params = dict(ffn1_norm=ffn1_norm, ffn1_w_in=ffn1_w_in, ffn1_w_out=ffn1_w_out, mix_norm=mix_norm, w_in=w_in,
                  gla_w_gate=gla_w_gate, gla_b_gate=gla_b_gate, gla_norm=gla_norm, pool_w=pool_w, pool_scale=pool_scale,
                  w_branch_a=w_branch_a, w_branch_b=w_branch_b, w_branch_c=w_branch_c, w_out=w_out,
                  ffn2_norm=ffn2_norm, ffn2_w_in=ffn2_w_in, ffn2_w_out=ffn2_w_out, ple_norm=ple_norm,
                  ple_w_gate=ple_w_gate, ple_w_proj=ple_w_proj)
    depth = w_in.shape[0]
    layers = [_layer_weights(i, params) for i in range(depth)]
    fnorm = final_norm.reshape(1, D_MODEL).astype(F32)
    prompt = _run_group(x_prompt, p_prompt, None, None, None, None, layers, fnorm)
    sample = _run_group(x_sample, p_sample, cache_sb_k, cache_sb_v, state_gla, state_pool, layers, fnorm)
    return (prompt[0], sample[0], prompt[1], prompt[2], prompt[3], prompt[4],
            sample[1], sample[2], sample[3], sample[4])
```

```python
import functools

import jax
import jax.numpy as jnp
from jax import lax
from jax.experimental import pallas as pl
from jax.experimental.pallas import tpu as pltpu

D_MODEL = 1024
FFN_DIM = 2816
PLE_DIM = 256
SB_HEADS = 8
SB_HEAD_DIM = 64
SB_WIDTH = SB_HEADS * SB_HEAD_DIM
GLA_HEADS = 4
GLA_KEY_DIM = 32
GLA_VAL_DIM = 64
GLA_KEY_WIDTH = GLA_HEADS * GLA_KEY_DIM
GLA_VAL_WIDTH = GLA_HEADS * GLA_VAL_DIM
GLA_GATE_RANK = 16
GLA_GATE_TAU = 16.0
GLA_CHUNK = 64
POOL_WINDOWS = (2, 4, 8, 16)
POOL_GROUP_DIM = 64
POOL_WIDTH = len(POOL_WINDOWS) * POOL_GROUP_DIM
POOL_HIST = 15
POOL_HIST_ROWS = 16
RMS_EPS = 1e-6

LANES = 128
VMEM_LIMIT_BYTES = 56 * 1024 * 1024
TOKEN_TILE = 512
FFN_CHUNK = 256
SB_QUERY_BLOCK = 128
SB_KEY_BLOCK = 128
SB_LOG_ZERO = -104.0

F32 = jnp.float32
BF16 = jnp.bfloat16


def _dot(a, b):
    return jnp.dot(a, b, preferred_element_type=F32)


def _dot_nt(a, b):
    return lax.dot_general(a, b, (((1,), (1,)), ((), ())), preferred_element_type=F32)


def _dot_tn(a, b):
    return lax.dot_general(a, b, (((0,), (0,)), ((), ())), preferred_element_type=F32)


def _dot_split(a, b):
    hi = a.astype(BF16)
    lo = (a - hi.astype(F32)).astype(BF16)
    return _dot(hi, b) + _dot(lo, b)


def _iota_div(shape, axis, size):
    assert size & (size - 1) == 0
    return lax.broadcasted_iota(jnp.int32, shape, axis) >> (size.bit_length() - 1)


def _rms(x, g):
    ms = jnp.mean(x * x, axis=-1, keepdims=True)
    return x * lax.rsqrt(ms + RMS_EPS) * g


def _params(*semantics):
    return pltpu.CompilerParams(dimension_semantics=semantics, vmem_limit_bytes=VMEM_LIMIT_BYTES)


def _const_spec(shape):
    zeros = (0,) * len(shape)
    return pl.BlockSpec(shape, lambda *_: zeros, pipeline_mode=pl.Buffered(1))


def _swiglu_update(x, norm_ref, wa_ref, wb_ref, wo_ref):
    h = _rms(x, norm_ref[...]).astype(BF16)
    acc = jnp.zeros(x.shape, F32)
    for f in range(0, FFN_DIM, FFN_CHUNK):
        a = _dot(h, wa_ref[:, f:f + FFN_CHUNK])
        b = _dot(h, wb_ref[:, f:f + FFN_CHUNK])
        g = (a * jax.nn.sigmoid(a) * b).astype(BF16)
        acc = acc + _dot(g, wo_ref[f:f + FFN_CHUNK, :])
    return x + 0.5 * acc


def _ffn_kernel(x_ref, norm_ref, wa_ref, wb_ref, wo_ref, o_ref):
    o_ref[...] = _swiglu_update(x_ref[...], norm_ref, wa_ref, wb_ref, wo_ref)


def _ffn_ple_kernel(x_ref, p_ref, norm_ref, wa_ref, wb_ref, wo_ref, pnorm_ref, wpg_ref, wpp_ref, fnorm_ref,
                    o_ref, *, final):
    x = _swiglu_update(x_ref[...], norm_ref, wa_ref, wb_ref, wo_ref)
    gate = jax.nn.sigmoid(_dot(_rms(x, pnorm_ref[...]).astype(BF16), wpg_ref[...]))
    x = x + gate * _dot(p_ref[...].astype(BF16), wpp_ref[...])
    if final:
        x = _rms(x, fnorm_ref[...])
    o_ref[...] = x


def _ffn(x, w):
    n = x.shape[0]
    tok = lambda i: (i, 0)
    return pl.pallas_call(
        _ffn_kernel,
        grid=(n // TOKEN_TILE,),
        in_specs=[pl.BlockSpec((TOKEN_TILE, D_MODEL), tok),
                  _const_spec((1, D_MODEL)),
                  _const_spec((D_MODEL, FFN_DIM)),
                  _const_spec((D_MODEL, FFN_DIM)),
                  _const_spec((FFN_DIM, D_MODEL))],
        out_specs=pl.BlockSpec((TOKEN_TILE, D_MODEL), tok),
        out_shape=jax.ShapeDtypeStruct((n, D_MODEL), F32),
        compiler_params=_params("parallel"),
        name="ffn",
    )(x, w["ffn1_norm"], w["ffn1_wa"], w["ffn1_wb"], w["ffn1_wo"])


def _ffn_ple(x, p, layer, w, final_norm, final):
    n = x.shape[0]
    tok = lambda i: (i, 0)
    return pl.pallas_call(
        functools.partial(_ffn_ple_kernel, final=final),
        grid=(n // TOKEN_TILE,),
        in_specs=[pl.BlockSpec((TOKEN_TILE, D_MODEL), tok),
                  pl.BlockSpec((None, TOKEN_TILE, PLE_DIM), lambda i: (layer, i, 0)),
                  _const_spec((1, D_MODEL)),
                  _const_spec((D_MODEL, FFN_DIM)),
                  _const_spec((D_MODEL, FFN_DIM)),
                  _const_spec((FFN_DIM, D_MODEL)),
                  _const_spec((1, D_MODEL)),
                  _const_spec((D_MODEL, D_MODEL)),
                  _const_spec((PLE_DIM, D_MODEL)),
                  _const_spec((1, D_MODEL))],
        out_specs=pl.BlockSpec((TOKEN_TILE, D_MODEL), tok),
        out_shape=jax.ShapeDtypeStruct((n, D_MODEL), F32),
        compiler_params=_params("parallel"),
        name="ffn_ple",
    )(x, p, w["ffn2_norm"], w["ffn2_wa"], w["ffn2_wb"], w["ffn2_wo"], w["ple_norm"], w["ple_wg"], w["ple_wp"],
      final_norm)


def _qkv_kernel(x_ref, norm_ref, w_ref, k_all_ref, v_all_ref, q_ref, kb_ref, vb_ref, k_ref, v_ref):
    del k_all_ref, v_all_ref
    h = _rms(x_ref[...], norm_ref[...]).astype(BF16)
    u = _dot(h, w_ref[...])
    q_ref[...] = (u[:, :SB_WIDTH] * SB_HEAD_DIM ** -0.5).astype(BF16)
    k = u[:, SB_WIDTH:2 * SB_WIDTH]
    v = u[:, 2 * SB_WIDTH:]
    for head in range(SB_HEADS):
        cols = slice(head * SB_HEAD_DIM, (head + 1) * SB_HEAD_DIM)
        k_ref[pl.ds(head, TOKEN_TILE, stride=SB_HEADS), :] = k[:, cols]
        v_ref[pl.ds(head, TOKEN_TILE, stride=SB_HEADS), :] = v[:, cols]
    kb_ref[...] = k.astype(BF16)
    vb_ref[...] = v.astype(BF16)


def _qkv(x, w, k_all, v_all, layer):
    n = x.shape[0]
    tok = lambda i: (i, 0)
    layer_tok = lambda i: (layer, i, 0)
    return pl.pallas_call(
        _qkv_kernel,
        grid=(n // TOKEN_TILE,),
        in_specs=[pl.BlockSpec((TOKEN_TILE, D_MODEL), tok),
                  _const_spec((1, D_MODEL)),
                  _const_spec((D_MODEL, 3 * SB_WIDTH)),
                  pl.BlockSpec(memory_space=pl.ANY),
                  pl.BlockSpec(memory_space=pl.ANY)],
        out_specs=[pl.BlockSpec((TOKEN_TILE, SB_WIDTH), tok)] * 3
        + [pl.BlockSpec((None, TOKEN_TILE * SB_HEADS, SB_HEAD_DIM), layer_tok)] * 2,
        out_shape=[jax.ShapeDtypeStruct((n, SB_WIDTH), BF16)] * 3
        + [jax.ShapeDtypeStruct(k_all.shape, F32), jax.ShapeDtypeStruct(v_all.shape, F32)],
        input_output_aliases={3: 3, 4: 4},
        compiler_params=_params("parallel"),
        name="qkv",
    )(x, w["mix_norm"], w["w_qkv"], k_all, v_all)


def _sb_blocks(chains, tri_ones, head_lanes):
    n = tri_ones.shape[0] // 2
    row = lax.broadcasted_iota(jnp.int32, (LANES, 1), 0)
    head_rows = (row < SB_HEAD_DIM, row >= SB_HEAD_DIM)
    log_beta, split = [], []
    for q2, blocks, _, _ in chains:
        for k, _, mask, transposed in blocks:
            for p in range(len(q2)):
                z = _dot(q2[p], k[p]) if transposed else _dot_nt(q2[p], k[p])
                sp = jnp.maximum(z, 0.0) + jnp.log(1.0 + jnp.exp(-jnp.abs(z)))
                log_beta.append(z - sp)
                if mask is not None:
                    sp = jnp.where(mask, sp, 0.0)
                hi = sp.astype(BF16)
                lo = (sp - hi.astype(F32)).astype(BF16)
                split.append(jnp.concatenate([hi, lo], axis=1))
    x = _dot(jnp.concatenate(split, axis=0), tri_ones)
    results = []
    visit = 0
    r0 = 0
    for q2, blocks, c, acc in chains:
        c, acc = list(c), list(acc)
        rows = acc[0].shape[0]
        for _, v, mask, transposed in blocks:
            for p in range(len(q2)):
                xs = x[r0:r0 + 2 * rows]
                w = jnp.exp(log_beta[visit] + xs[:, :n] + c[p])
                if mask is not None:
                    w = jnp.where(mask, w, 0.0)
                c[p] = c[p] + xs[:, n:]
                w2 = jnp.concatenate([w[:rows], w[rows:]], axis=1).astype(BF16)
                zero = jnp.zeros_like(v[p])
                if transposed:
                    v2 = jnp.concatenate([jnp.where(m, v[p], zero) for m in head_rows], axis=1)
                    acc[p] = acc[p] + _dot_nt(w2, v2)
                else:
                    v2 = jnp.concatenate([jnp.where(m, v[p], zero) for m in head_lanes], axis=0)
                    acc[p] = acc[p] + _dot(w2, v2)
                visit += 1
                r0 += 2 * rows
        results.append((c, acc))
    return results


def _sb_kernel(*refs, qb, n_cache, layer):
    n = SB_KEY_BLOCK
    pairs = SB_WIDTH // LANES
    if n_cache:
        (q_ref, kn_ref, vn_ref, kwin_ref, vwin_ref, kc_hbm, vc_hbm, o_ref, acc_s, c_s, kold_s, vold_s,
         old_sem) = refs
        assert n_cache % n == 0 and n_cache >= 2 * n and kn_ref.shape[1] == qb
    else:
        q_ref, kn_ref, vn_ref, o_ref, acc_s, c_s = refs
        assert qb == n
    bi = pl.program_id(0)
    i = pl.program_id(1)

    lane = lax.broadcasted_iota(jnp.int32, (1, LANES), 1)
    head_lanes = (lane < SB_HEAD_DIM, lane >= SB_HEAD_DIM)
    tri_row = lax.broadcasted_iota(jnp.int32, (2 * n, 2 * n), 0) & (n - 1)
    tri_col = lax.broadcasted_iota(jnp.int32, (2 * n, 2 * n), 1)
    tri_ones = -jnp.logical_or(tri_col >= n, tri_row > tri_col).astype(BF16)
    key_s = lax.broadcasted_iota(jnp.int32, (1, n), 1)
    row_t = lax.broadcasted_iota(jnp.int32, (2 * qb, 1), 0) & (qb - 1)
    causal = key_s < row_t

    def lanes(p):
        return slice(p * LANES, (p + 1) * LANES)

    def per_pair(ref, rows):
        return [ref[0, rows, lanes(p)].astype(BF16) for p in range(pairs)]

    def per_pair_t(ref, cols):
        return [ref[lanes(p), cols].astype(BF16) for p in range(pairs)]

    def stacked_q(rows=slice(None)):
        q2 = []
        for p in range(pairs):
            qp = q_ref[0, rows, lanes(p)]
            q2.append(jnp.concatenate([jnp.where(m, qp, jnp.zeros_like(qp)) for m in head_lanes], axis=0))
        return q2

    def first_block():
        own = pl.ds(pl.multiple_of(i * qb, qb), qb)
        ks, vs = per_pair(kn_ref, own), per_pair(vn_ref, own)
        if qb < n:
            pad = jnp.zeros((n - qb, LANES), BF16)
            ks = [jnp.concatenate([x, pad], axis=0) for x in ks]
            vs = [jnp.concatenate([x, pad], axis=0) for x in vs]
        return ks, vs, causal, False

    def next_block(d):
        if n_cache:
            cols = slice((2 - d) * n, (3 - d) * n)
            return per_pair_t(kwin_ref, cols), per_pair_t(vwin_ref, cols), None, True
        rows = pl.ds(pl.multiple_of((i - d) * n, n), n)
        return per_pair(kn_ref, rows), per_pair(vn_ref, rows), None, False

    def older_block(start, mask):
        if n_cache:
            cols = pl.ds(start, n)
            copies = [pltpu.make_async_copy(kc_hbm.at[layer, bi, :, cols], kold_s, old_sem.at[0]),
                      pltpu.make_async_copy(vc_hbm.at[layer, bi, :, cols], vold_s, old_sem.at[1])]
            for cp in copies:
                cp.start()
            for cp in copies:
                cp.wait()
            return per_pair_t(kold_s, slice(None)), per_pair_t(vold_s, slice(None)), mask, True
        rows = pl.ds(start, n)
        return per_pair(kn_ref, rows), per_pair(vn_ref, rows), mask, False

    def run(blocks, fresh_start):
        if fresh_start:
            c = [jnp.zeros((2 * qb, n), F32)] * pairs
            acc = [jnp.zeros((qb, LANES), F32)] * pairs
        else:
            c = [c_s[p] for p in range(pairs)]
            acc = [acc_s[p] for p in range(pairs)]
        ((c, acc),) = _sb_blocks([(stacked_q(), blocks, c, acc)], tri_ones, head_lanes)
        for p in range(pairs):
            c_s[p], acc_s[p] = c[p], acc[p]

    half = qb // 2

    def two_halves():
        in_reach = key_s < (row_t & (half - 1))[:2 * half] + (n - half)
        chains = []
        for u in range(2):
            first = pl.multiple_of(i * qb + (u + 1) * half - n, half)
            before = pl.multiple_of(first - n, half)
            blocks = [(per_pair(kn_ref, pl.ds(first, n)), per_pair(vn_ref, pl.ds(first, n)), in_reach, False),
                      (per_pair(kn_ref, pl.ds(before, n)), per_pair(vn_ref, pl.ds(before, n)), None, False)]
            chains.append((stacked_q(slice(u * half, (u + 1) * half)), blocks,
                           [jnp.zeros((2 * half, n), F32)] * pairs, [jnp.zeros((half, LANES), F32)] * pairs))
        for u, (c, acc) in enumerate(_sb_blocks(chains, tri_ones, head_lanes)):
            for p in range(pairs):
                for head in range(2):
                    c_s[p, head * qb + u * half:head * qb + (u + 1) * half, :] = c[p][head * half:(head + 1) * half]
                acc_s[p, u * half:(u + 1) * half, :] = acc[p]

    if n_cache:
        run([first_block(), next_block(1), next_block(2)], True)
        limit_rows = jnp.full((2 * qb, 1), n_cache - 2 * n, jnp.int32)
        unvisited = jnp.int32(n_cache - 2 * n)
    else:
        pl.when(i >= 2)(two_halves)
        pl.when(i < 2)(lambda: run([first_block()], True))
        second_half = (row_t >= half).astype(jnp.int32)
        limit_rows = jnp.where(i >= 2, i * qb + (second_half + 1) * half - 2 * n, i * qb)
        unvisited = jnp.where(i >= 2, i * qb + 2 * half - 2 * n, i * qb)

    def live(rem):
        return jnp.logical_and(rem > 0, jnp.max(c_s[...]) >= SB_LOG_ZERO)

    def one_block(rem):
        start = pl.multiple_of(rem - n, n)
        run([older_block(start, start + key_s < limit_rows)], False)
        return start

    lax.while_loop(live, one_block, unvisited)
    o_ref[0] = jnp.concatenate([acc_s[p] for p in range(pairs)], axis=1).astype(o_ref.dtype)


def _sb_attention(q, k_new, v_new, k_cache, v_cache, layer):
    b, t, _ = q.shape
    n_cache = 0 if k_cache is None else k_cache.shape[3]
    qb = min(SB_QUERY_BLOCK, t)
    assert t % qb == 0 and (n_cache == 0 or t == qb)
    n = SB_KEY_BLOCK
    pairs = SB_WIDTH // LANES
    blk = lambda bi, i: (bi, i, 0)
    full = lambda bi, i: (bi, 0, 0)
    in_specs = [pl.BlockSpec((1, qb, SB_WIDTH), blk),
                pl.BlockSpec((1, t, SB_WIDTH), full),
                pl.BlockSpec((1, t, SB_WIDTH), full)]
    args = [q, k_new, v_new]
    scratch = [pltpu.VMEM((pairs, qb, LANES), F32), pltpu.VMEM((pairs, 2 * qb, n), F32)]
    if n_cache:
        newest = lambda bi, i: (layer, bi, 0, n_cache // (2 * n) - 1)
        in_specs += [pl.BlockSpec((None, None, SB_WIDTH, 2 * n), newest)] * 2 + [pl.BlockSpec(memory_space=pl.ANY)] * 2
        args += [k_cache, v_cache, k_cache, v_cache]
        scratch += [pltpu.VMEM((SB_WIDTH, n), F32), pltpu.VMEM((SB_WIDTH, n), F32), pltpu.SemaphoreType.DMA((2,))]
    return pl.pallas_call(
        functools.partial(_sb_kernel, qb=qb, n_cache=n_cache, layer=layer),
        grid=(b, t // qb),
        in_specs=in_specs,
        out_specs=pl.BlockSpec((1, qb, SB_WIDTH), blk),
        out_shape=jax.ShapeDtypeStruct((b, t, SB_WIDTH), BF16),
        scratch_shapes=scratch,
        compiler_params=_params("parallel", "arbitrary"),
        name="sb_attention",
    )(*args)


def _gla_tile(q, k, v, g, st0, masks, chunks_per_seq):
    tril, causal, k_heads, v_heads, st_heads = masks
    m = q.shape[0]
    c = GLA_CHUNK
    g_hi = g.astype(BF16)
    g_lo = (g - g_hi.astype(F32)).astype(BF16)
    bs, mids, lasts = [], [], []
    for ci in range(m // c):
        r = slice(ci * c, (ci + 1) * c)
        b_c = _dot(tril, g_hi[r]) + _dot(tril, g_lo[r])
        bs.append(b_c)
        mids.append(jnp.broadcast_to(b_c[c // 2 - 1:c // 2], b_c.shape))
        lasts.append(jnp.broadcast_to(b_c[c - 1:c], b_c.shape))
    b, mid, last = (jnp.concatenate(x, axis=0) for x in (bs, mids, lasts))
    q_in = (q * jnp.exp(b - mid)).astype(BF16)
    k_in = k * jnp.exp(mid - b)
    q_dec = (q * jnp.exp(b)).astype(BF16)
    k_out = (k * jnp.exp(last - b)).astype(BF16)
    decay = jnp.exp(last)
    v_bf = v.astype(BF16)
    intra, updates = [], []
    for ci in range(m // c):
        r = slice(ci * c, (ci + 1) * c)
        k_rows = (jnp.concatenate([k_in[r]] * GLA_HEADS, axis=0) * k_heads).astype(BF16)
        v_rows = jnp.concatenate([v_bf[r]] * GLA_HEADS, axis=0) * v_heads
        scores = jnp.where(causal, _dot_nt(q_in[r], k_rows), 0.0).astype(BF16)
        intra.append(_dot(scores, v_rows))
        updates.append(_dot_tn(v_bf[r], k_out[r]) * st_heads)
    o_parts, st_new = [], []
    for s, st in enumerate(st0):
        for j in range(chunks_per_seq):
            ci = s * chunks_per_seq + j
            r = slice(ci * c, (ci + 1) * c)
            o_parts.append(_dot_nt(q_dec[r], st.astype(BF16)) + intra[ci])
            st = st * decay[ci * c:ci * c + 1] + updates[ci]
        st_new.append(st)
    return jnp.concatenate(o_parts, axis=0), st_new


def _mixer_kernel(x_ref, ya_ref, st0_ref, hist0_ref, norm_ref, wp_ref, wgate_ref, glawg_ref, glabg_ref, glanorm_ref,
                  poolw_ref, poolscale_ref, wa_ref, wb_ref, wc_ref, wout_ref,
                  o_ref, st_ref, hist_ref, st_s, ext_s, *, seqs, rows, n_past):
    t = pl.program_id(1)
    m = seqs * rows

    @pl.when(t == 0)
    def _():
        st_s[...] = st0_ref[...]
        ext_s[:, 0:POOL_HIST_ROWS, :] = hist0_ref[...]

    x = x_ref[...].reshape(m, D_MODEL)
    h = _rms(x, norm_ref[...]).astype(BF16)
    u = _dot(h, wp_ref[...])
    kw, vw = GLA_KEY_WIDTH, GLA_VAL_WIDTH
    q_b = u[:, 0:kw] * GLA_KEY_DIM ** -0.5
    k_b = u[:, kw:2 * kw]
    v_b = u[:, 2 * kw:2 * kw + vw]
    r_b = u[:, 2 * kw + vw:3 * kw + vw]
    o_b = u[:, 3 * kw + vw:3 * kw + 2 * vw]
    u_c = u[:, 3 * kw + 2 * vw:]

    gate = _dot(r_b.astype(BF16), glawg_ref[...]) + glabg_ref[...]
    log_alpha = (jnp.minimum(gate, 0.0) - jnp.log1p(jnp.exp(-jnp.abs(gate)))) / GLA_GATE_TAU
    c = GLA_CHUNK
    ri = lax.broadcasted_iota(jnp.int32, (c, c), 0)
    ci = lax.broadcasted_iota(jnp.int32, (c, c), 1)
    tril = (ci <= ri).astype(BF16)
    key_head = _iota_div((1, kw), 1, GLA_KEY_DIM)
    val_head = _iota_div((1, vw), 1, GLA_VAL_DIM)
    rows_hs = _iota_div((GLA_HEADS * c, 1), 0, c)
    causal = (lax.broadcasted_iota(jnp.int32, (c, GLA_HEADS * c), 1) & (c - 1)) <= lax.broadcasted_iota(
        jnp.int32, (c, GLA_HEADS * c), 0)
    k_heads = (rows_hs == key_head).astype(F32)
    v_heads = (rows_hs == val_head).astype(BF16)
    st_heads = (_iota_div((vw, 1), 0, GLA_VAL_DIM) == key_head).astype(F32)
    masks = (tril, causal, k_heads, v_heads, st_heads)
    o, st_new = _gla_tile(q_b, k_b, v_b, log_alpha, [st_s[s] for s in range(seqs)], masks, rows // c)
    for s in range(seqs):
        st_s[s] = st_new[s]
    head_mean = (_iota_div((vw, 1), 0, GLA_VAL_DIM) == val_head).astype(BF16) * (1.0 / GLA_VAL_DIM)
    ms = _dot_split(o * o, head_mean)
    y_b = o * lax.rsqrt(ms + RMS_EPS) * glanorm_ref[...] * (o_b * jax.nn.sigmoid(o_b))

    lane = lax.broadcasted_iota(jnp.int32, (1, POOL_WIDTH), 1)
    pos1 = n_past + t * rows + 1 + lax.broadcasted_iota(jnp.int32, (rows, 1), 0)
    d_parts = []
    for s in range(seqs):
        ext_s[s, POOL_HIST_ROWS:, :] = u_c[s * rows:(s + 1) * rows]
        e = ext_s[s]
        sums = []
        span = 1
        for w in POOL_WINDOWS:
            while span < w:
                e = e + pltpu.roll(e, span, 0)
                span *= 2
            sums.append(e[POOL_HIST_ROWS:])
        wsum = sums[-1]
        cnt = jnp.minimum(POOL_WINDOWS[-1], pos1)
        for gi in range(len(POOL_WINDOWS) - 2, -1, -1):
            in_group = lane < (gi + 1) * POOL_GROUP_DIM
            wsum = jnp.where(in_group, sums[gi], wsum)
            cnt = jnp.where(in_group, jnp.minimum(POOL_WINDOWS[gi], pos1), cnt)
        d_parts.append(wsum / cnt.astype(F32) - u_c[s * rows:(s + 1) * rows])
        ext_s[s, 0:POOL_HIST_ROWS, :] = ext_s[s, rows:rows + POOL_HIST_ROWS, :]
    d = jnp.concatenate(d_parts, axis=0)
    y_c = _dot(d.astype(BF16), poolw_ref[...]) * poolscale_ref[...]

    branches = ((ya_ref[...].reshape(m, SB_WIDTH), wa_ref), (y_b.astype(BF16), wb_ref), (y_c.astype(BF16), wc_ref))
    merged = None
    for bi, (y, w_ref) in enumerate(branches):
        gate_b = jax.nn.sigmoid(_dot(h, wgate_ref[:, bi * D_MODEL:(bi + 1) * D_MODEL]))
        term = gate_b * _dot(y, w_ref[...])
        merged = term if merged is None else merged + term
    o_ref[...] = (x + _dot(merged.astype(BF16), wout_ref[...])).reshape(seqs, rows, D_MODEL)

    @pl.when(t == pl.num_programs(1) - 1)
    def _():
        st_ref[...] = st_s[...]
        hist_ref[...] = ext_s[:, 0:POOL_HIST_ROWS, :]


def _mixer(x, y_a, st0, hist0, w, n_past):
    b, t, _ = x.shape
    rows = min(TOKEN_TILE, t)
    seqs = TOKEN_TILE // rows
    assert t % rows == 0 and b % seqs == 0 and rows % GLA_CHUNK == 0
    tok = lambda bi, ti: (bi, ti, 0)
    per_seq = lambda bi, ti: (bi, 0, 0)
    kw, vw = GLA_KEY_WIDTH, GLA_VAL_WIDTH
    return pl.pallas_call(
        functools.partial(_mixer_kernel, seqs=seqs, rows=rows, n_past=n_past),
        grid=(b // seqs, t // rows),
        in_specs=[pl.BlockSpec((seqs, rows, D_MODEL), tok),
                  pl.BlockSpec((seqs, rows, SB_WIDTH), tok),
                  pl.BlockSpec((seqs, vw, kw), per_seq),
                  pl.BlockSpec((seqs, POOL_HIST_ROWS, POOL_WIDTH), per_seq),
                  _const_spec((1, D_MODEL)),
                  _const_spec((D_MODEL, 3 * kw + 3 * vw)),
                  _const_spec((D_MODEL, 3 * D_MODEL)),
                  _const_spec((kw, kw)),
                  _const_spec((1, kw)),
                  _const_spec((1, vw)),
                  _const_spec((POOL_WIDTH, POOL_WIDTH)),
                  _const_spec((1, POOL_WIDTH)),
                  _const_spec((SB_WIDTH, D_MODEL)),
                  _const_spec((vw, D_MODEL)),
                  _const_spec((POOL_WIDTH, D_MODEL)),
                  _const_spec((D_MODEL, D_MODEL))],
        out_specs=[pl.BlockSpec((seqs, rows, D_MODEL), tok),
                   pl.BlockSpec((seqs, vw, kw), per_seq),
                   pl.BlockSpec((seqs, POOL_HIST_ROWS, POOL_WIDTH), per_seq)],
        out_shape=[jax.ShapeDtypeStruct((b, t, D_MODEL), F32),
                   jax.ShapeDtypeStruct((b, vw, kw), F32),
                   jax.ShapeDtypeStruct((b, POOL_HIST_ROWS, POOL_WIDTH), F32)],
        scratch_shapes=[pltpu.VMEM((seqs, vw, kw), F32),
                        pltpu.VMEM((seqs, POOL_HIST_ROWS + rows, POOL_WIDTH), F32)],
        compiler_params=_params("parallel", "arbitrary"),
        name="mixer",
    )(x, y_a, st0, hist0, w["mix_norm"], w["w_proj"], w["w_gate"], w["gla_wg"], w["gla_bg"], w["gla_norm"],
      w["pool_w"], w["pool_scale"], w["w_branch_a"], w["w_branch_b"], w["w_branch_c"], w["w_out"])


def _layer_weights(i, p):
    row = lambda a: a[i].reshape(1, -1).astype(F32)
    bf = lambda a: a.astype(BF16)
    kw, vw = GLA_KEY_WIDTH, GLA_VAL_WIDTH
    w_in = p["w_in"][i]
    a_end = 3 * SB_WIDTH
    qkv_b = w_in[:, a_end:a_end + 2 * kw + vw]
    r0 = a_end + 2 * kw + vw
    r_b = jnp.pad(w_in[:, r0:r0 + GLA_GATE_RANK], ((0, 0), (0, kw - GLA_GATE_RANK)))
    o0 = r0 + GLA_GATE_RANK
    rest = w_in[:, o0:o0 + vw + POOL_WIDTH]
    g0 = o0 + vw + POOL_WIDTH
    pool_w = jnp.zeros((POOL_WIDTH, POOL_WIDTH), F32)
    for gi in range(len(POOL_WINDOWS)):
        lo = gi * POOL_GROUP_DIM
        pool_w = pool_w.at[lo:lo + POOL_GROUP_DIM, lo:lo + POOL_GROUP_DIM].set(p["pool_w"][i, gi])
    return dict(
        ffn1_norm=row(p["ffn1_norm"]), ffn1_wa=bf(p["ffn1_w_in"][i, :, :FFN_DIM]), ffn1_wb=bf(p["ffn1_w_in"][i, :, FFN_DIM:]),
        ffn1_wo=bf(p["ffn1_w_out"][i]),
        mix_norm=row(p["mix_norm"]), w_qkv=bf(w_in[:, :a_end]),
        w_proj=bf(jnp.concatenate([qkv_b, r_b, rest], axis=1)), w_gate=bf(w_in[:, g0:]),
        gla_wg=bf(jnp.pad(p["gla_w_gate"][i], ((0, kw - GLA_GATE_RANK), (0, 0)))),
        gla_bg=row(p["gla_b_gate"]), gla_norm=row(p["gla_norm"]),
        pool_w=bf(pool_w), pool_scale=row(p["pool_scale"]),
        w_branch_a=bf(p["w_branch_a"][i]), w_branch_b=bf(p["w_branch_b"][i]), w_branch_c=bf(p["w_branch_c"][i]),
        w_out=bf(p["w_out"][i]),
        ffn2_norm=row(p["ffn2_norm"]), ffn2_wa=bf(p["ffn2_w_in"][i, :, :FFN_DIM]), ffn2_wb=bf(p["ffn2_w_in"][i, :, FFN_DIM:]),
        ffn2_wo=bf(p["ffn2_w_out"][i]),
        ple_norm=row(p["ple_norm"]), ple_wg=bf(p["ple_w_gate"][i]), ple_wp=bf(p["ple_w_proj"][i]),
    )


def _state_to_blocks(s):
    b = s.shape[0]
    eye = jnp.eye(GLA_HEADS, dtype=s.dtype)
    blocks = jnp.einsum("bhde,hg->bhegd", s, eye)
    return blocks.reshape(b, GLA_VAL_WIDTH, GLA_KEY_WIDTH)


def _blocks_to_state(st):
    b = st.shape[0]
    blocks = st.reshape(b, GLA_HEADS, GLA_VAL_DIM, GLA_HEADS, GLA_KEY_DIM)
    diag = jnp.stack([blocks[:, h, :, h, :] for h in range(GLA_HEADS)], axis=1)
    return diag.transpose(0, 1, 3, 2)


def _run_group(x, p, k_cache, v_cache, s_gla, s_pool, layers, final_norm):
    b, t, _ = x.shape
    n = b * t
    depth = len(layers)
    has_cache = k_cache is not None
    n_past = k_cache.shape[2] if has_cache else 0
    if has_cache:
        k_cache, v_cache = (a.transpose(0, 1, 3, 4, 2).reshape(depth, b, SB_WIDTH, n_past) for a in (k_cache, v_cache))
    x = x.reshape(n, D_MODEL)
    k_all = jnp.zeros((depth, n * SB_HEADS, SB_HEAD_DIM), F32)
    v_all = jnp.zeros((depth, n * SB_HEADS, SB_HEAD_DIM), F32)
    ss, ps = [], []
    for i, w in enumerate(layers):
        x = _ffn(x, w)
        q, k, v, k_all, v_all = _qkv(x, w, k_all, v_all, i)
        q, k3, v3 = (a.reshape(b, t, SB_WIDTH) for a in (q, k, v))
        if has_cache:
            st0 = _state_to_blocks(s_gla[i])
            hist0 = jnp.pad(s_pool[i], ((0, 0), (POOL_HIST_ROWS - POOL_HIST, 0), (0, 0)))
        else:
            st0 = jnp.zeros((b, GLA_VAL_WIDTH, GLA_KEY_WIDTH), F32)
            hist0 = jnp.zeros((b, POOL_HIST_ROWS, POOL_WIDTH), F32)
        y_a = _sb_attention(q, k3, v3, k_cache, v_cache, i)
        x3, st, hist = _mixer(x.reshape(b, t, D_MODEL), y_a, st0, hist0, w, n_past)
        x = _ffn_ple(x3.reshape(n, D_MODEL), p.reshape(depth, n, PLE_DIM), i, w, final_norm, final=(i == depth - 1))
        ss.append(_blocks_to_state(st))
        ps.append(hist[:, POOL_HIST_ROWS - POOL_HIST:, :])
    heads = (depth, b, t, SB_HEADS, SB_HEAD_DIM)
    return x.reshape(b, t, D_MODEL), k_all.reshape(heads), v_all.reshape(heads), jnp.stack(ss), jnp.stack(ps)


def kernel(x_prompt, x_sample, cache_sb_k, cache_sb_v, state_gla, state_pool, p_prompt, p_sample, ffn1_norm, ffn1_w_in, ffn1_w_out, mix_norm, w_in, gla_w_gate, gla_b_gate, gla_norm, pool_w, pool_scale, w_branch_a, w_branch_b, w_branch_c, w_out, ffn2_norm, ffn2_w_in, ffn2_w_out, ple_norm, ple_w_gate, ple_w_proj, final_norm):
    params = dict(ffn1_norm=ffn1_norm, ffn1_w_in=ffn1_w_in, ffn1_w_out=ffn1_w_out, mix_norm=mix_norm, w_in=w_in,
                  gla_w_gate=gla_w_gate, gla_b_gate=gla_b_gate, gla_norm=gla_norm, pool_w=pool_w, pool_scale=pool_scale,
                  w_branch_a=w_branch_a, w_branch_b=w_branch_b, w_branch_c=w_branch_c, w_out=w_out,
                  ffn2_norm=ffn2_norm, ffn2_w_in=ffn2_w_in, ffn2_w_out=ffn2_w_out, ple_norm=ple_norm,
                  ple_w_gate=ple_w_gate, ple_w_proj=ple_w_proj)
    depth = w_in.shape[0]
    layers = [_layer_weights(i, params) for i in range(depth)]
    fnorm = final_norm.reshape(1, D_MODEL).astype(F32)
    prompt = _run_group(x_prompt, p_prompt, None, None, None, None, layers, fnorm)
    sample = _run_group(x_sample, p_sample, cache_sb_k, cache_sb_v, state_gla, state_pool, layers, fnorm)
    return (prompt[0], sample[0], prompt[1], prompt[2], prompt[3], prompt[4],
            sample[1], sample[2], sample[3], sample[4])
```

```python
import functools

import jax
import jax.numpy as jnp
from jax import lax
from jax.experimental import pallas as pl
from jax.experimental.pallas import tpu as pltpu

D_MODEL = 1024
FFN_DIM = 2816
PLE_DIM = 256
SB_HEADS = 8
SB_HEAD_DIM = 64
SB_WIDTH = SB_HEADS * SB_HEAD_DIM
GLA_HEADS = 4
GLA_KEY_DIM = 32
GLA_VAL_DIM = 64
GLA_KEY_WIDTH = GLA_HEADS * GLA_KEY_DIM
GLA_VAL_WIDTH = GLA_HEADS * GLA_VAL_DIM
GLA_GATE_RANK = 16
GLA_GATE_TAU = 16.0
GLA_CHUNK = 64
POOL_WINDOWS = (2, 4, 8, 16)
POOL_GROUP_DIM = 64
POOL_WIDTH = len(POOL_WINDOWS) * POOL_GROUP_DIM
POOL_HIST = 15
POOL_HIST_ROWS = 16
RMS_EPS = 1e-6

LANES = 128
VMEM_LIMIT_BYTES = 56 * 1024 * 1024
TOKEN_TILE = 512
FFN_CHUNK = 256
SB_QUERY_BLOCK = 128
SB_KEY_BLOCK = 128
SB_LOG_ZERO = -104.0

F32 = jnp.float32
BF16 = jnp.bfloat16


def _dot(a, b):
    return jnp.dot(a, b, preferred_element_type=F32)


def _dot_nt(a, b):
    return lax.dot_general(a, b, (((1,), (1,)), ((), ())), preferred_element_type=F32)


def _dot_tn(a, b):
    return lax.dot_general(a, b, (((0,), (0,)), ((), ())), preferred_element_type=F32)


def _dot_split(a, b):
    hi = a.astype(BF16)
    lo = (a - hi.astype(F32)).astype(BF16)
    return _dot(hi, b) + _dot(lo, b)


def _iota_div(shape, axis, size):
    assert size & (size - 1) == 0
    return lax.broadcasted_iota(jnp.int32, shape, axis) >> (size.bit_length() - 1)


def _rms(x, g):
    ms = jnp.mean(x * x, axis=-1, keepdims=True)
    return x * lax.rsqrt(ms + RMS_EPS) * g


def _params(*semantics):
    return pltpu.CompilerParams(dimension_semantics=semantics, vmem_limit_bytes=VMEM_LIMIT_BYTES)


def _const_spec(shape):
    zeros = (0,) * len(shape)
    return pl.BlockSpec(shape, lambda *_: zeros, pipeline_mode=pl.Buffered(1))


def _swiglu_update(x, norm_ref, wa_ref, wb_ref, wo_ref):
    h = _rms(x, norm_ref[...]).astype(BF16)
    acc = jnp.zeros(x.shape, F32)
    for f in range(0, FFN_DIM, FFN_CHUNK):
        a = _dot(h, wa_ref[:, f:f + FFN_CHUNK])
        b = _dot(h, wb_ref[:, f:f + FFN_CHUNK])
        g = (a * jax.nn.sigmoid(a) * b).astype(BF16)
        acc = acc + _dot(g, wo_ref[f:f + FFN_CHUNK, :])
    return x + 0.5 * acc


def _ffn_ple_kernel(x_ref, p_ref, norm_ref, wa_ref, wb_ref, wo_ref, pnorm_ref, wpg_ref, wpp_ref, fnorm_ref,
                    o_ref, *, final):
    x = _swiglu_update(x_ref[...], norm_ref, wa_ref, wb_ref, wo_ref)
    gate = jax.nn.sigmoid(_dot(_rms(x, pnorm_ref[...]).astype(BF16), wpg_ref[...]))
    x = x + gate * _dot(p_ref[...].astype(BF16), wpp_ref[...])
    if final:
        x = _rms(x, fnorm_ref[...])
    o_ref[...] = x


def _ffn_ple(x, p, layer, w, final_norm, final):
    n = x.shape[0]
    tok = lambda i: (i, 0)
    return pl.pallas_call(
        functools.partial(_ffn_ple_kernel, final=final),
        grid=(n // TOKEN_TILE,),
        in_specs=[pl.BlockSpec((TOKEN_TILE, D_MODEL), tok),
                  pl.BlockSpec((None, TOKEN_TILE, PLE_DIM), lambda i: (layer, i, 0)),
                  _const_spec((1, D_MODEL)),
                  _const_spec((D_MODEL, FFN_DIM)),
                  _const_spec((D_MODEL, FFN_DIM)),
                  _const_spec((FFN_DIM, D_MODEL)),
                  _const_spec((1, D_MODEL)),
                  _const_spec((D_MODEL, D_MODEL)),
                  _const_spec((PLE_DIM, D_MODEL)),
                  _const_spec((1, D_MODEL))],
        out_specs=pl.BlockSpec((TOKEN_TILE, D_MODEL), tok),
        out_shape=jax.ShapeDtypeStruct((n, D_MODEL), F32),
        compiler_params=_params("parallel"),
        name="ffn_ple",
    )(x, p, w["ffn2_norm"], w["ffn2_wa"], w["ffn2_wb"], w["ffn2_wo"], w["ple_norm"], w["ple_wg"], w["ple_wp"],
      final_norm)


def _ffn_qkv_kernel(x_ref, norm_ref, wa_ref, wb_ref, wo_ref, mnorm_ref, w_ref, k_all_ref, v_all_ref,
                    o_ref, q_ref, kb_ref, vb_ref, k_ref, v_ref):
    del k_all_ref, v_all_ref
    x = _swiglu_update(x_ref[...], norm_ref, wa_ref, wb_ref, wo_ref)
    o_ref[...] = x
    u = _dot(_rms(x, mnorm_ref[...]).astype(BF16), w_ref[...])
    q_ref[...] = (u[:, :SB_WIDTH] * SB_HEAD_DIM ** -0.5).astype(BF16)
    k = u[:, SB_WIDTH:2 * SB_WIDTH]
    v = u[:, 2 * SB_WIDTH:]
    for head in range(SB_HEADS):
        cols = slice(head * SB_HEAD_DIM, (head + 1) * SB_HEAD_DIM)
        k_ref[pl.ds(head, TOKEN_TILE, stride=SB_HEADS), :] = k[:, cols]
        v_ref[pl.ds(head, TOKEN_TILE, stride=SB_HEADS), :] = v[:, cols]
    kb_ref[...] = k.astype(BF16)
    vb_ref[...] = v.astype(BF16)


def _ffn_qkv(x, w, k_all, v_all, layer):
    n = x.shape[0]
    tok = lambda i: (i, 0)
    layer_tok = lambda i: (layer, i, 0)
    return pl.pallas_call(
        _ffn_qkv_kernel,
        grid=(n // TOKEN_TILE,),
        in_specs=[pl.BlockSpec((TOKEN_TILE, D_MODEL), tok),
                  _const_spec((1, D_MODEL)),
                  _const_spec((D_MODEL, FFN_DIM)),
                  _const_spec((D_MODEL, FFN_DIM)),
                  _const_spec((FFN_DIM, D_MODEL)),
                  _const_spec((1, D_MODEL)),
                  _const_spec((D_MODEL, 3 * SB_WIDTH)),
                  pl.BlockSpec(memory_space=pl.ANY),
                  pl.BlockSpec(memory_space=pl.ANY)],
        out_specs=[pl.BlockSpec((TOKEN_TILE, D_MODEL), tok)] + [pl.BlockSpec((TOKEN_TILE, SB_WIDTH), tok)] * 3
        + [pl.BlockSpec((None, TOKEN_TILE * SB_HEADS, SB_HEAD_DIM), layer_tok)] * 2,
        out_shape=[jax.ShapeDtypeStruct((n, D_MODEL), F32)] + [jax.ShapeDtypeStruct((n, SB_WIDTH), BF16)] * 3
        + [jax.ShapeDtypeStruct(k_all.shape, F32), jax.ShapeDtypeStruct(v_all.shape, F32)],
        input_output_aliases={7: 4, 8: 5},
        compiler_params=_params("parallel"),
        name="ffn_qkv",
    )(x, w["ffn1_norm"], w["ffn1_wa"], w["ffn1_wb"], w["ffn1_wo"], w["mix_norm"], w["w_qkv"], k_all, v_all)


def _sb_blocks(chains, tri_ones, head_lanes):
    n = tri_ones.shape[0] // 2
    row = lax.broadcasted_iota(jnp.int32, (LANES, 1), 0)
    head_rows = (row < SB_HEAD_DIM, row >= SB_HEAD_DIM)
    log_beta, split = [], []
    for q2, blocks, _, _ in chains:
        for k, _, mask, transposed in blocks:
            for p in range(len(q2)):
                z = _dot(q2[p], k[p]) if transposed else _dot_nt(q2[p], k[p])
                sp = jnp.maximum(z, 0.0) + jnp.log(1.0 + jnp.exp(-jnp.abs(z)))
                log_beta.append(z - sp)
                if mask is not None:
                    sp = jnp.where(mask, sp, 0.0)
                hi = sp.astype(BF16)
                lo = (sp - hi.astype(F32)).astype(BF16)
                split.append(jnp.concatenate([hi, lo], axis=1))
    x = _dot(jnp.concatenate(split, axis=0), tri_ones)
    results = []
    visit = 0
    r0 = 0
    for q2, blocks, c, acc in chains:
        c, acc = list(c), list(acc)
        rows = acc[0].shape[0]
        for _, v, mask, transposed in blocks:
            for p in range(len(q2)):
                xs = x[r0:r0 + 2 * rows]
                w = jnp.exp(log_beta[visit] + xs[:, :n] + c[p])
                if mask is not None:
                    w = jnp.where(mask, w, 0.0)
                c[p] = c[p] + xs[:, n:]
                w2 = jnp.concatenate([w[:rows], w[rows:]], axis=1).astype(BF16)
                zero = jnp.zeros_like(v[p])
                if transposed:
                    v2 = jnp.concatenate([jnp.where(m, v[p], zero) for m in head_rows], axis=1)
                    acc[p] = acc[p] + _dot_nt(w2, v2)
                else:
                    v2 = jnp.concatenate([jnp.where(m, v[p], zero) for m in head_lanes], axis=0)
                    acc[p] = acc[p] + _dot(w2, v2)
                visit += 1
                r0 += 2 * rows
        results.append((c, acc))
    return results


def _sb_kernel(*refs, qb, n_cache, layer):
    n = SB_KEY_BLOCK
    pairs = SB_WIDTH // LANES
    if n_cache:
        (q_ref, kn_ref, vn_ref, kwin_ref, vwin_ref, kc_hbm, vc_hbm, o_ref, acc_s, c_s, kold_s, vold_s,
         old_sem) = refs
        assert n_cache % n == 0 and n_cache >= 2 * n and kn_ref.shape[1] == qb
    else:
        q_ref, kn_ref, vn_ref, o_ref, acc_s, c_s = refs
        assert qb == n
    bi = pl.program_id(0)
    i = pl.program_id(1)

    lane = lax.broadcasted_iota(jnp.int32, (1, LANES), 1)
    head_lanes = (lane < SB_HEAD_DIM, lane >= SB_HEAD_DIM)
    tri_row = lax.broadcasted_iota(jnp.int32, (2 * n, 2 * n), 0) & (n - 1)
    tri_col = lax.broadcasted_iota(jnp.int32, (2 * n, 2 * n), 1)
    tri_ones = -jnp.logical_or(tri_col >= n, tri_row > tri_col).astype(BF16)
    key_s = lax.broadcasted_iota(jnp.int32, (1, n), 1)
    row_t = lax.broadcasted_iota(jnp.int32, (2 * qb, 1), 0) & (qb - 1)
    causal = key_s < row_t

    def lanes(p):
        return slice(p * LANES, (p + 1) * LANES)

    def per_pair(ref, rows):
        return [ref[0, rows, lanes(p)].astype(BF16) for p in range(pairs)]

    def per_pair_t(ref, cols):
        return [ref[lanes(p), cols].astype(BF16) for p in range(pairs)]

    def stacked_q(rows=slice(None)):
        q2 = []
        for p in range(pairs):
            qp = q_ref[0, rows, lanes(p)]
            q2.append(jnp.concatenate([jnp.where(m, qp, jnp.zeros_like(qp)) for m in head_lanes], axis=0))
        return q2

    def first_block():
        own = pl.ds(pl.multiple_of(i * qb, qb), qb)
        ks, vs = per_pair(kn_ref, own), per_pair(vn_ref, own)
        if qb < n:
            pad = jnp.zeros((n - qb, LANES), BF16)
            ks = [jnp.concatenate([x, pad], axis=0) for x in ks]
            vs = [jnp.concatenate([x, pad], axis=0) for x in vs]
        return ks, vs, causal, False

    def next_block(d):
        if n_cache:
            cols = slice((2 - d) * n, (3 - d) * n)
            return per_pair_t(kwin_ref, cols), per_pair_t(vwin_ref, cols), None, True
        rows = pl.ds(pl.multiple_of((i - d) * n, n), n)
        return per_pair(kn_ref, rows), per_pair(vn_ref, rows), None, False

    def older_block(start, mask):
        if n_cache:
            cols = pl.ds(start, n)
            copies = [pltpu.make_async_copy(kc_hbm.at[layer, bi, :, cols], kold_s, old_sem.at[0]),
                      pltpu.make_async_copy(vc_hbm.at[layer, bi, :, cols], vold_s, old_sem.at[1])]
            for cp in copies:
                cp.start()
            for cp in copies:
                cp.wait()
            return per_pair_t(kold_s, slice(None)), per_pair_t(vold_s, slice(None)), mask, True
        rows = pl.ds(start, n)
        return per_pair(kn_ref, rows), per_pair(vn_ref, rows), mask, False

    def run(blocks, fresh_start):
        if fresh_start:
            c = [jnp.zeros((2 * qb, n), F32)] * pairs
            acc = [jnp.zeros((qb, LANES), F32)] * pairs
        else:
            c = [c_s[p] for p in range(pairs)]
            acc = [acc_s[p] for p in range(pairs)]
        ((c, acc),) = _sb_blocks([(stacked_q(), blocks, c, acc)], tri_ones, head_lanes)
        for p in range(pairs):
            c_s[p], acc_s[p] = c[p], acc[p]

    half = qb // 2

    def two_halves():
        in_reach = key_s < (row_t & (half - 1))[:2 * half] + (n - half)
        chains = []
        for u in range(2):
            first = pl.multiple_of(i * qb + (u + 1) * half - n, half)
            before = pl.multiple_of(first - n, half)
            blocks = [(per_pair(kn_ref, pl.ds(first, n)), per_pair(vn_ref, pl.ds(first, n)), in_reach, False),
                      (per_pair(kn_ref, pl.ds(before, n)), per_pair(vn_ref, pl.ds(before, n)), None, False)]
            chains.append((stacked_q(slice(u * half, (u + 1) * half)), blocks,
                           [jnp.zeros((2 * half, n), F32)] * pairs, [jnp.zeros((half, LANES), F32)] * pairs))
        for u, (c, acc) in enumerate(_sb_blocks(chains, tri_ones, head_lanes)):
            for p in range(pairs):
                for head in range(2):
                    c_s[p, head * qb + u * half:head * qb + (u + 1) * half, :] = c[p][head * half:(head + 1) * half]
                acc_s[p, u * half:(u + 1) * half, :] = acc[p]

    if n_cache:
        run([first_block(), next_block(1), next_block(2)], True)
        limit_rows = jnp.full((2 * qb, 1), n_cache - 2 * n, jnp.int32)
        unvisited = jnp.int32(n_cache - 2 * n)
    else:
        pl.when(i >= 2)(two_halves)
        pl.when(i < 2)(lambda: run([first_block()], True))
        second_half = (row_t >= half).astype(jnp.int32)
        limit_rows = jnp.where(i >= 2, i * qb + (second_half + 1) * half - 2 * n, i * qb)
        unvisited = jnp.where(i >= 2, i * qb + 2 * half - 2 * n, i * qb)

    def live(rem):
        return jnp.logical_and(rem > 0, jnp.max(c_s[...]) >= SB_LOG_ZERO)

    def one_block(rem):
        start = pl.multiple_of(rem - n, n)
        run([older_block(start, start + key_s < limit_rows)], False)
        return start

    lax.while_loop(live, one_block, unvisited)
    o_ref[0] = jnp.concatenate([acc_s[p] for p in range(pairs)], axis=1).astype(o_ref.dtype)


def _sb_attention(q, k_new, v_new, k_cache, v_cache, layer):
    b, t, _ = q.shape
    n_cache = 0 if k_cache is None else k_cache.shape[3]
    qb = min(SB_QUERY_BLOCK, t)
    assert t % qb == 0 and (n_cache == 0 or t == qb)
    n = SB_KEY_BLOCK
    pairs = SB_WIDTH // LANES
    blk = lambda bi, i: (bi, i, 0)
    full = lambda bi, i: (bi, 0, 0)
    in_specs = [pl.BlockSpec((1, qb, SB_WIDTH), blk),
                pl.BlockSpec((1, t, SB_WIDTH), full),
                pl.BlockSpec((1, t, SB_WIDTH), full)]
    args = [q, k_new, v_new]
    scratch = [pltpu.VMEM((pairs, qb, LANES), F32), pltpu.VMEM((pairs, 2 * qb, n), F32)]
    if n_cache:
        newest = lambda bi, i: (layer, bi, 0, n_cache // (2 * n) - 1)
        in_specs += [pl.BlockSpec((None, None, SB_WIDTH, 2 * n), newest)] * 2 + [pl.BlockSpec(memory_space=pl.ANY)] * 2
        args += [k_cache, v_cache, k_cache, v_cache]
        scratch += [pltpu.VMEM((SB_WIDTH, n), F32), pltpu.VMEM((SB_WIDTH, n), F32), pltpu.SemaphoreType.DMA((2,))]
    return pl.pallas_call(
        functools.partial(_sb_kernel, qb=qb, n_cache=n_cache, layer=layer),
        grid=(b, t // qb),
        in_specs=in_specs,
        out_specs=pl.BlockSpec((1, qb, SB_WIDTH), blk),
        out_shape=jax.ShapeDtypeStruct((b, t, SB_WIDTH), BF16),
        scratch_shapes=scratch,
        compiler_params=_params("parallel", "arbitrary"),
        name="sb_attention",
    )(*args)


def _gla_tile(q, k, v, g, st0, masks, chunks_per_seq):
    tril, causal, k_heads, v_heads, st_heads = masks
    m = q.shape[0]
    c = GLA_CHUNK
    g_hi = g.astype(BF16)
    g_lo = (g - g_hi.astype(F32)).astype(BF16)
    bs, mids, lasts = [], [], []
    for ci in range(m // c):
        r = slice(ci * c, (ci + 1) * c)
        b_c = _dot(tril, g_hi[r]) + _dot(tril, g_lo[r])
        bs.append(b_c)
        mids.append(jnp.broadcast_to(b_c[c // 2 - 1:c // 2], b_c.shape))
        lasts.append(jnp.broadcast_to(b_c[c - 1:c], b_c.shape))
    b, mid, last = (jnp.concatenate(x, axis=0) for x in (bs, mids, lasts))
    q_in = (q * jnp.exp(b - mid)).astype(BF16)
    k_in = k * jnp.exp(mid - b)
    q_dec = (q * jnp.exp(b)).astype(BF16)
    k_out = (k * jnp.exp(last - b)).astype(BF16)
    decay = jnp.exp(last)
    v_bf = v.astype(BF16)
    intra, updates = [], []
    for ci in range(m // c):
        r = slice(ci * c, (ci + 1) * c)
        k_rows = (jnp.concatenate([k_in[r]] * GLA_HEADS, axis=0) * k_heads).astype(BF16)
        v_rows = jnp.concatenate([v_bf[r]] * GLA_HEADS, axis=0) * v_heads
        scores = jnp.where(causal, _dot_nt(q_in[r], k_rows), 0.0).astype(BF16)
        intra.append(_dot(scores, v_rows))
        updates.append(_dot_tn(v_bf[r], k_out[r]) * st_heads)
    o_parts, st_new = [], []
    for s, st in enumerate(st0):
        for j in range(chunks_per_seq):
            ci = s * chunks_per_seq + j
            r = slice(ci * c, (ci + 1) * c)
            o_parts.append(_dot_nt(q_dec[r], st.astype(BF16)) + intra[ci])
            st = st * decay[ci * c:ci * c + 1] + updates[ci]
        st_new.append(st)
    return jnp.concatenate(o_parts, axis=0), st_new


def _mixer_kernel(x_ref, ya_ref, st0_ref, hist0_ref, norm_ref, wp_ref, wgate_ref, glawg_ref, glabg_ref, glanorm_ref,
                  poolw_ref, poolscale_ref, wa_ref, wb_ref, wc_ref, wout_ref,
                  o_ref, st_ref, hist_ref, st_s, ext_s, *, seqs, rows, n_past):
    t = pl.program_id(1)
    m = seqs * rows

    @pl.when(t == 0)
    def _():
        st_s[...] = st0_ref[...]
        ext_s[:, 0:POOL_HIST_ROWS, :] = hist0_ref[...]

    x = x_ref[...].reshape(m, D_MODEL)
    h = _rms(x, norm_ref[...]).astype(BF16)
    u = _dot(h, wp_ref[...])
    kw, vw = GLA_KEY_WIDTH, GLA_VAL_WIDTH
    q_b = u[:, 0:kw] * GLA_KEY_DIM ** -0.5
    k_b = u[:, kw:2 * kw]
    v_b = u[:, 2 * kw:2 * kw + vw]
    r_b = u[:, 2 * kw + vw:3 * kw + vw]
    o_b = u[:, 3 * kw + vw:3 * kw + 2 * vw]
    u_c = u[:, 3 * kw + 2 * vw:]

    gate = _dot(r_b.astype(BF16), glawg_ref[...]) + glabg_ref[...]
    log_alpha = (jnp.minimum(gate, 0.0) - jnp.log1p(jnp.exp(-jnp.abs(gate)))) / GLA_GATE_TAU
    c = GLA_CHUNK
    ri = lax.broadcasted_iota(jnp.int32, (c, c), 0)
    ci = lax.broadcasted_iota(jnp.int32, (c, c), 1)
    tril = (ci <= ri).astype(BF16)
    key_head = _iota_div((1, kw), 1, GLA_KEY_DIM)
    val_head = _iota_div((1, vw), 1, GLA_VAL_DIM)
    rows_hs = _iota_div((GLA_HEADS * c, 1), 0, c)
    causal = (lax.broadcasted_iota(jnp.int32, (c, GLA_HEADS * c), 1) & (c - 1)) <= lax.broadcasted_iota(
        jnp.int32, (c, GLA_HEADS * c), 0)
    k_heads = (rows_hs == key_head).astype(F32)
    v_heads = (rows_hs == val_head).astype(BF16)
    st_heads = (_iota_div((vw, 1), 0, GLA_VAL_DIM) == key_head).astype(F32)
    masks = (tril, causal, k_heads, v_heads, st_heads)
    o, st_new = _gla_tile(q_b, k_b, v_b, log_alpha, [st_s[s] for s in range(seqs)], masks, rows // c)
    for s in range(seqs):
        st_s[s] = st_new[s]
    head_mean = (_iota_div((vw, 1), 0, GLA_VAL_DIM) == val_head).astype(BF16) * (1.0 / GLA_VAL_DIM)
    ms = _dot_split(o * o, head_mean)
    y_b = o * lax.rsqrt(ms + RMS_EPS) * glanorm_ref[...] * (o_b * jax.nn.sigmoid(o_b))

    lane = lax.broadcasted_iota(jnp.int32, (1, POOL_WIDTH), 1)
    pos1 = n_past + t * rows + 1 + lax.broadcasted_iota(jnp.int32, (rows, 1), 0)
    d_parts = []
    for s in range(seqs):
        ext_s[s, POOL_HIST_ROWS:, :] = u_c[s * rows:(s + 1) * rows]
        e = ext_s[s]
        sums = []
        span = 1
        for w in POOL_WINDOWS:
            while span < w:
                e = e + pltpu.roll(e, span, 0)
                span *= 2
            sums.append(e[POOL_HIST_ROWS:])
        wsum = sums[-1]
        cnt = jnp.minimum(POOL_WINDOWS[-1], pos1)
        for gi in range(len(POOL_WINDOWS) - 2, -1, -1):
            in_group = lane < (gi + 1) * POOL_GROUP_DIM
            wsum = jnp.where(in_group, sums[gi], wsum)
            cnt = jnp.where(in_group, jnp.minimum(POOL_WINDOWS[gi], pos1), cnt)
        d_parts.append(wsum / cnt.astype(F32) - u_c[s * rows:(s + 1) * rows])
        ext_s[s, 0:POOL_HIST_ROWS, :] = ext_s[s, rows:rows + POOL_HIST_ROWS, :]
    d = jnp.concatenate(d_parts, axis=0)
    y_c = _dot(d.astype(BF16), poolw_ref[...]) * poolscale_ref[...]

    branches = ((ya_ref[...].reshape(m, SB_WIDTH), wa_ref), (y_b.astype(BF16), wb_ref), (y_c.astype(BF16), wc_ref))
    merged = None
    for bi, (y, w_ref) in enumerate(branches):
        gate_b = jax.nn.sigmoid(_dot(h, wgate_ref[:, bi * D_MODEL:(bi + 1) * D_MODEL]))
        term = gate_b * _dot(y, w_ref[...])
        merged = term if merged is None else merged + term
    o_ref[...] = (x + _dot(merged.astype(BF16), wout_ref[...])).reshape(seqs, rows, D_MODEL)

    @pl.when(t == pl.num_programs(1) - 1)
    def _():
        st_ref[...] = st_s[...]
        hist_ref[...] = ext_s[:, 0:POOL_HIST_ROWS, :]


def _mixer(x, y_a, st0, hist0, w, n_past):
    b, t, _ = x.shape
    rows = min(TOKEN_TILE, t)
    seqs = TOKEN_TILE // rows
    assert t % rows == 0 and b % seqs == 0 and rows % GLA_CHUNK == 0
    tok = lambda bi, ti: (bi, ti, 0)
    per_seq = lambda bi, ti: (bi, 0, 0)
    kw, vw = GLA_KEY_WIDTH, GLA_VAL_WIDTH
    return pl.pallas_call(
        functools.partial(_mixer_kernel, seqs=seqs, rows=rows, n_past=n_past),
        grid=(b // seqs, t // rows),
        in_specs=[pl.BlockSpec((seqs, rows, D_MODEL), tok),
                  pl.BlockSpec((seqs, rows, SB_WIDTH), tok),
                  pl.BlockSpec((seqs, vw, kw), per_seq),
                  pl.BlockSpec((seqs, POOL_HIST_ROWS, POOL_WIDTH), per_seq),
                  _const_spec((1, D_MODEL)),
                  _const_spec((D_MODEL, 3 * kw + 3 * vw)),
                  _const_spec((D_MODEL, 3 * D_MODEL)),
                  _const_spec((kw, kw)),
                  _const_spec((1, kw)),
                  _const_spec((1, vw)),
                  _const_spec((POOL_WIDTH, POOL_WIDTH)),
                  _const_spec((1, POOL_WIDTH)),
                  _const_spec((SB_WIDTH, D_MODEL)),
                  _const_spec((vw, D_MODEL)),
                  _const_spec((POOL_WIDTH, D_MODEL)),
                  _const_spec((D_MODEL, D_MODEL))],
        out_specs=[pl.BlockSpec((seqs, rows, D_MODEL), tok),
                   pl.BlockSpec((seqs, vw, kw), per_seq),
                   pl.BlockSpec((seqs, POOL_HIST_ROWS, POOL_WIDTH), per_seq)],
        out_shape=[jax.ShapeDtypeStruct((b, t, D_MODEL), F32),
                   jax.ShapeDtypeStruct((b, vw, kw), F32),
                   jax.ShapeDtypeStruct((b, POOL_HIST_ROWS, POOL_WIDTH), F32)],
        scratch_shapes=[pltpu.VMEM((seqs, vw, kw), F32),
                        pltpu.VMEM((seqs, POOL_HIST_ROWS + rows, POOL_WIDTH), F32)],
        compiler_params=_params("parallel", "arbitrary"),
        name="mixer",
    )(x, y_a, st0, hist0, w["mix_norm"], w["w_proj"], w["w_gate"], w["gla_wg"], w["gla_bg"], w["gla_norm"],
      w["pool_w"], w["pool_scale"], w["w_branch_a"], w["w_branch_b"], w["w_branch_c"], w["w_out"])


def _layer_weights(i, p):
    row = lambda a: a[i].reshape(1, -1).astype(F32)
    bf = lambda a: a.astype(BF16)
    kw, vw = GLA_KEY_WIDTH, GLA_VAL_WIDTH
    w_in = p["w_in"][i]
    a_end = 3 * SB_WIDTH
    qkv_b = w_in[:, a_end:a_end + 2 * kw + vw]
    r0 = a_end + 2 * kw + vw
    r_b = jnp.pad(w_in[:, r0:r0 + GLA_GATE_RANK], ((0, 0), (0, kw - GLA_GATE_RANK)))
    o0 = r0 + GLA_GATE_RANK
    rest = w_in[:, o0:o0 + vw + POOL_WIDTH]
    g0 = o0 + vw + POOL_WIDTH
    pool_w = jnp.zeros((POOL_WIDTH, POOL_WIDTH), F32)
    for gi in range(len(POOL_WINDOWS)):
        lo = gi * POOL_GROUP_DIM
        pool_w = pool_w.at[lo:lo + POOL_GROUP_DIM, lo:lo + POOL_GROUP_DIM].set(p["pool_w"][i, gi])
    return dict(
        ffn1_norm=row(p["ffn1_norm"]), ffn1_wa=bf(p["ffn1_w_in"][i, :, :FFN_DIM]), ffn1_wb=bf(p["ffn1_w_in"][i, :, FFN_DIM:]),
        ffn1_wo=bf(p["ffn1_w_out"][i]),
        mix_norm=row(p["mix_norm"]), w_qkv=bf(w_in[:, :a_end]),
        w_proj=bf(jnp.concatenate([qkv_b, r_b, rest], axis=1)), w_gate=bf(w_in[:, g0:]),
        gla_wg=bf(jnp.pad(p["gla_w_gate"][i], ((0, kw - GLA_GATE_RANK), (0, 0)))),
        gla_bg=row(p["gla_b_gate"]), gla_norm=row(p["gla_norm"]),
        pool_w=bf(pool_w), pool_scale=row(p["pool_scale"]),
        w_branch_a=bf(p["w_branch_a"][i]), w_branch_b=bf(p["w_branch_b"][i]), w_branch_c=bf(p["w_branch_c"][i]),
        w_out=bf(p["w_out"][i]),
        ffn2_norm=row(p["ffn2_norm"]), ffn2_wa=bf(p["ffn2_w_in"][i, :, :FFN_DIM]), ffn2_wb=bf(p["ffn2_w_in"][i, :, FFN_DIM:]),
        ffn2_wo=bf(p["ffn2_w_out"][i]),
        ple_norm=row(p["ple_norm"]), ple_wg=bf(p["ple_w_gate"][i]), ple_wp=bf(p["ple_w_proj"][i]),
    )


def _state_to_blocks(s):
    b = s.shape[0]
    eye = jnp.eye(GLA_HEADS, dtype=s.dtype)
    blocks = jnp.einsum("bhde,hg->bhegd", s, eye)
    return blocks.reshape(b, GLA_VAL_WIDTH, GLA_KEY_WIDTH)


def _blocks_to_state(st):
    b = st.shape[0]
    blocks = st.reshape(b, GLA_HEADS, GLA_VAL_DIM, GLA_HEADS, GLA_KEY_DIM)
    diag = jnp.stack([blocks[:, h, :, h, :] for h in range(GLA_HEADS)], axis=1)
    return diag.transpose(0, 1, 3, 2)


def _run_group(x, p, k_cache, v_cache, s_gla, s_pool, layers, final_norm):
    b, t, _ = x.shape
    n = b * t
    depth = len(layers)
    has_cache = k_cache is not None
    n_past = k_cache.shape[2] if has_cache else 0
    if has_cache:
        k_cache, v_cache = (a.transpose(0, 1, 3, 4, 2).reshape(depth, b, SB_WIDTH, n_past) for a in (k_cache, v_cache))
    x = x.reshape(n, D_MODEL)
    k_all = jnp.zeros((depth, n * SB_HEADS, SB_HEAD_DIM), F32)
    v_all = jnp.zeros((depth, n * SB_HEADS, SB_HEAD_DIM), F32)
    ss, ps = [], []
    for i, w in enumerate(layers):
        x, q, k, v, k_all, v_all = _ffn_qkv(x, w, k_all, v_all, i)
        q, k3, v3 = (a.reshape(b, t, SB_WIDTH) for a in (q, k, v))
        if has_cache:
            st0 = _state_to_blocks(s_gla[i])
            hist0 = jnp.pad(s_pool[i], ((0, 0), (POOL_HIST_ROWS - POOL_HIST, 0), (0, 0)))
        else:
            st0 = jnp.zeros((b, GLA_VAL_WIDTH, GLA_KEY_WIDTH), F32)
            hist0 = jnp.zeros((b, POOL_HIST_ROWS, POOL_WIDTH), F32)
        y_a = _sb_attention(q, k3, v3, k_cache, v_cache, i)
        x3, st, hist = _mixer(x.reshape(b, t, D_MODEL), y_a, st0, hist0, w, n_past)
        x = _ffn_ple(x3.reshape(n, D_MODEL), p.reshape(depth, n, PLE_DIM), i, w, final_norm, final=(i == depth - 1))
        ss.append(_blocks_to_state(st))
        ps.append(hist[:, POOL_HIST_ROWS - POOL_HIST:, :])
    heads = (depth, b, t, SB_HEADS, SB_HEAD_DIM)
    return x.reshape(b, t, D_MODEL), k_all.reshape(heads), v_all.reshape(heads), jnp.stack(ss), jnp.stack(ps)


def kernel(x_prompt, x_sample, cache_sb_k, cache_sb_v, state_gla, state_pool, p_prompt, p_sample, ffn1_norm, ffn1_w_in, ffn1_w_out, mix_norm, w_in, gla_w_gate, gla_b_gate, gla_norm, pool_w, pool_scale, w_branch_a, w_branch_b, w_branch_c, w_out, ffn2_norm, ffn2_w_in, ffn2_w_out, ple_norm, ple_w_gate, ple_w_proj, final_norm):
    params = dict(ffn1_norm=ffn1_norm, ffn1_w_in=ffn1_w_in, ffn1_w_out=ffn1_w_out, mix_norm=mix_norm, w_in=w_in,
                  gla_w_gate=gla_w_gate, gla_b_gate=gla_b_gate, gla_norm=gla_norm, pool_w=pool_w, pool_scale=pool_scale,
                  w_branch_a=w_branch_a, w_branch_b=w_branch_b, w_branch_c=w_branch_c, w_out=w_out,
                  ffn2_norm=ffn2_norm, ffn2_w_in=ffn2_w_in, ffn2_w_out=ffn2_w_out, ple_norm=ple_norm,
                  ple_w_gate=ple_w_gate, ple_w_proj=ple_w_proj)
    depth = w_in.shape[0]
    layers = [_layer_weights(i, params) for i in range(depth)]
    fnorm = final_norm.reshape(1, D_MODEL).astype(F32)
    prompt = _run_group(x_prompt, p_prompt, None, None, None, None, layers, fnorm)
    sample = _run_group(x_sample, p_sample, cache_sb_k, cache_sb_v, state_gla, state_pool, layers, fnorm)
    return (prompt[0], sample[0], prompt[1], prompt[2], prompt[3], prompt[4],
            sample[1], sample[2], sample[3], sample[4])
```

```python
import functools

import jax
import jax.numpy as jnp
from jax import lax
from jax.experimental import pallas as pl
from jax.experimental.pallas import tpu as pltpu

D_MODEL = 1024
FFN_DIM = 2816
PLE_DIM = 256
SB_HEADS = 8
SB_HEAD_DIM = 64
SB_WIDTH = SB_HEADS * SB_HEAD_DIM
GLA_HEADS = 4
GLA_KEY_DIM = 32
GLA_VAL_DIM = 64
GLA_KEY_WIDTH = GLA_HEADS * GLA_KEY_DIM
GLA_VAL_WIDTH = GLA_HEADS * GLA_VAL_DIM
GLA_GATE_RANK = 16
GLA_GATE_TAU = 16.0
GLA_CHUNK = 64
POOL_WINDOWS = (2, 4, 8, 16)
POOL_GROUP_DIM = 64
POOL_WIDTH = len(POOL_WINDOWS) * POOL_GROUP_DIM
POOL_HIST = 15
POOL_HIST_ROWS = 16
RMS_EPS = 1e-6

LANES = 128
VMEM_LIMIT_BYTES = 56 * 1024 * 1024
TOKEN_TILE = 512
FFN_CHUNK = 256
SB_QUERY_BLOCK = 128
SB_KEY_BLOCK = 128
SB_LOG_ZERO = -104.0

F32 = jnp.float32
BF16 = jnp.bfloat16


def _dot(a, b):
    return jnp.dot(a, b, preferred_element_type=F32)


def _dot_nt(a, b):
    return lax.dot_general(a, b, (((1,), (1,)), ((), ())), preferred_element_type=F32)


def _dot_tn(a, b):
    return lax.dot_general(a, b, (((0,), (0,)), ((), ())), preferred_element_type=F32)


def _dot_split(a, b):
    hi = a.astype(BF16)
    lo = (a - hi.astype(F32)).astype(BF16)
    return _dot(hi, b) + _dot(lo, b)


def _iota_div(shape, axis, size):
    assert size & (size - 1) == 0
    return lax.broadcasted_iota(jnp.int32, shape, axis) >> (size.bit_length() - 1)


def _rms(x, g):
    ms = jnp.mean(x * x, axis=-1, keepdims=True)
    return x * lax.rsqrt(ms + RMS_EPS) * g


def _params(*semantics):
    return pltpu.CompilerParams(dimension_semantics=semantics, vmem_limit_bytes=VMEM_LIMIT_BYTES)


def _const_spec(shape):
    zeros = (0,) * len(shape)
    return pl.BlockSpec(shape, lambda *_: zeros, pipeline_mode=pl.Buffered(1))


def _swiglu_update(x, norm_ref, wa_ref, wb_ref, wo_ref):
    h = _rms(x, norm_ref[...]).astype(BF16)
    acc = jnp.zeros(x.shape, F32)
    for f in range(0, FFN_DIM, FFN_CHUNK):
        a = _dot(h, wa_ref[:, f:f + FFN_CHUNK])
        b = _dot(h, wb_ref[:, f:f + FFN_CHUNK])
        g = (a * jax.nn.sigmoid(a) * b).astype(BF16)
        acc = acc + _dot(g, wo_ref[f:f + FFN_CHUNK, :])
    return x + 0.5 * acc


def _ffn_ple_kernel(x_ref, p_ref, norm_ref, wa_ref, wb_ref, wo_ref, pnorm_ref, wpg_ref, wpp_ref, fnorm_ref,
                    o_ref, *, final):
    x = _swiglu_update(x_ref[...], norm_ref, wa_ref, wb_ref, wo_ref)
    gate = jax.nn.sigmoid(_dot(_rms(x, pnorm_ref[...]).astype(BF16), wpg_ref[...]))
    x = x + gate * _dot(p_ref[...].astype(BF16), wpp_ref[...])
    if final:
        x = _rms(x, fnorm_ref[...])
    o_ref[...] = x


def _ffn_ple(x, p, layer, w, final_norm, final):
    n = x.shape[0]
    tok = lambda i: (i, 0)
    return pl.pallas_call(
        functools.partial(_ffn_ple_kernel, final=final),
        grid=(n // TOKEN_TILE,),
        in_specs=[pl.BlockSpec((TOKEN_TILE, D_MODEL), tok),
                  pl.BlockSpec((None, TOKEN_TILE, PLE_DIM), lambda i: (layer, i, 0)),
                  _const_spec((1, D_MODEL)),
                  _const_spec((D_MODEL, FFN_DIM)),
                  _const_spec((D_MODEL, FFN_DIM)),
                  _const_spec((FFN_DIM, D_MODEL)),
                  _const_spec((1, D_MODEL)),
                  _const_spec((D_MODEL, D_MODEL)),
                  _const_spec((PLE_DIM, D_MODEL)),
                  _const_spec((1, D_MODEL))],
        out_specs=pl.BlockSpec((TOKEN_TILE, D_MODEL), tok),
        out_shape=jax.ShapeDtypeStruct((n, D_MODEL), F32),
        compiler_params=_params("parallel"),
        name="ffn_ple",
    )(x, p, w["ffn2_norm"], w["ffn2_wa"], w["ffn2_wb"], w["ffn2_wo"], w["ple_norm"], w["ple_wg"], w["ple_wp"],
      final_norm)


def _ffn_qkv_kernel(*refs, layer, first):
    if first:
        x_ref, norm_ref, wa_ref, wb_ref, wo_ref, mnorm_ref, w_ref, o_ref, q_ref, kb_ref, vb_ref, k_ref, v_ref = refs
        for ref in (k_ref, v_ref):
            for other in range(ref.shape[0]):
                if other != layer:
                    ref[other] = jnp.zeros(ref.shape[1:], ref.dtype)
        k_ref, v_ref = k_ref.at[layer], v_ref.at[layer]
    else:
        (x_ref, norm_ref, wa_ref, wb_ref, wo_ref, mnorm_ref, w_ref, _, _,
         o_ref, q_ref, kb_ref, vb_ref, k_ref, v_ref) = refs
    x = _swiglu_update(x_ref[...], norm_ref, wa_ref, wb_ref, wo_ref)
    o_ref[...] = x
    u = _dot(_rms(x, mnorm_ref[...]).astype(BF16), w_ref[...])
    q_ref[...] = (u[:, :SB_WIDTH] * SB_HEAD_DIM ** -0.5).astype(BF16)
    k = u[:, SB_WIDTH:2 * SB_WIDTH]
    v = u[:, 2 * SB_WIDTH:]
    for head in range(SB_HEADS):
        cols = slice(head * SB_HEAD_DIM, (head + 1) * SB_HEAD_DIM)
        k_ref[pl.ds(head, TOKEN_TILE, stride=SB_HEADS), :] = k[:, cols]
        v_ref[pl.ds(head, TOKEN_TILE, stride=SB_HEADS), :] = v[:, cols]
    kb_ref[...] = k.astype(BF16)
    vb_ref[...] = v.astype(BF16)


def _ffn_qkv(x, w, k_all, v_all, layer, depth):
    n = x.shape[0]
    first = k_all is None
    tok = lambda i: (i, 0)
    kv_rows = TOKEN_TILE * SB_HEADS
    kv_shape = jax.ShapeDtypeStruct((depth, n * SB_HEADS, SB_HEAD_DIM), F32)
    if first:
        kv_spec = pl.BlockSpec((depth, kv_rows, SB_HEAD_DIM), lambda i: (0, i, 0))
        extra_specs, extra_args, aliases = [], [], {}
    else:
        kv_spec = pl.BlockSpec((None, kv_rows, SB_HEAD_DIM), lambda i: (layer, i, 0))
        extra_specs, extra_args, aliases = [pl.BlockSpec(memory_space=pl.ANY)] * 2, [k_all, v_all], {7: 4, 8: 5}
    return pl.pallas_call(
        functools.partial(_ffn_qkv_kernel, layer=layer, first=first),
        grid=(n // TOKEN_TILE,),
        in_specs=[pl.BlockSpec((TOKEN_TILE, D_MODEL), tok),
                  _const_spec((1, D_MODEL)),
                  _const_spec((D_MODEL, FFN_DIM)),
                  _const_spec((D_MODEL, FFN_DIM)),
                  _const_spec((FFN_DIM, D_MODEL)),
                  _const_spec((1, D_MODEL)),
                  _const_spec((D_MODEL, 3 * SB_WIDTH))] + extra_specs,
        out_specs=[pl.BlockSpec((TOKEN_TILE, D_MODEL), tok)] + [pl.BlockSpec((TOKEN_TILE, SB_WIDTH), tok)] * 3
        + [kv_spec] * 2,
        out_shape=[jax.ShapeDtypeStruct((n, D_MODEL), F32)] + [jax.ShapeDtypeStruct((n, SB_WIDTH), BF16)] * 3
        + [kv_shape] * 2,
        input_output_aliases=aliases,
        compiler_params=_params("parallel"),
        name="ffn_qkv",
    )(x, w["ffn1_norm"], w["ffn1_wa"], w["ffn1_wb"], w["ffn1_wo"], w["mix_norm"], w["w_qkv"], *extra_args)


def _sb_blocks(chains, tri_ones, head_lanes):
    n = tri_ones.shape[0] // 2
    row = lax.broadcasted_iota(jnp.int32, (LANES, 1), 0)
    head_rows = (row < SB_HEAD_DIM, row >= SB_HEAD_DIM)
    log_beta, split = [], []
    for q2, blocks, _, _ in chains:
        for k, _, mask, transposed in blocks:
            for p in range(len(q2)):
                z = _dot(q2[p], k[p]) if transposed else _dot_nt(q2[p], k[p])
                sp = jnp.maximum(z, 0.0) + jnp.log(1.0 + jnp.exp(-jnp.abs(z)))
                log_beta.append(z - sp)
                if mask is not None:
                    sp = jnp.where(mask, sp, 0.0)
                hi = sp.astype(BF16)
                lo = (sp - hi.astype(F32)).astype(BF16)
                split.append(jnp.concatenate([hi, lo], axis=1))
    x = _dot(jnp.concatenate(split, axis=0), tri_ones)
    results = []
    visit = 0
    r0 = 0
    for q2, blocks, c, acc in chains:
        c, acc = list(c), list(acc)
        rows = acc[0].shape[0]
        for _, v, mask, transposed in blocks:
            for p in range(len(q2)):
                xs = x[r0:r0 + 2 * rows]
                w = jnp.exp(log_beta[visit] + xs[:, :n] + c[p])
                if mask is not None:
                    w = jnp.where(mask, w, 0.0)
                c[p] = c[p] + xs[:, n:]
                w2 = jnp.concatenate([w[:rows], w[rows:]], axis=1).astype(BF16)
                zero = jnp.zeros_like(v[p])
                if transposed:
                    v2 = jnp.concatenate([jnp.where(m, v[p], zero) for m in head_rows], axis=1)
                    acc[p] = acc[p] + _dot_nt(w2, v2)
                else:
                    v2 = jnp.concatenate([jnp.where(m, v[p], zero) for m in head_lanes], axis=0)
                    acc[p] = acc[p] + _dot(w2, v2)
                visit += 1
                r0 += 2 * rows
        results.append((c, acc))
    return results


def _sb_kernel(*refs, qb, n_cache, layer):
    n = SB_KEY_BLOCK
    pairs = SB_WIDTH // LANES
    if n_cache:
        (q_ref, kn_ref, vn_ref, kwin_ref, vwin_ref, kc_hbm, vc_hbm, o_ref, acc_s, c_s, kold_s, vold_s,
         old_sem) = refs
        assert n_cache % n == 0 and n_cache >= 2 * n and kn_ref.shape[1] == qb
    else:
        q_ref, kn_ref, vn_ref, o_ref, acc_s, c_s = refs
        assert qb == n
    bi = pl.program_id(0)
    i = pl.program_id(1)

    lane = lax.broadcasted_iota(jnp.int32, (1, LANES), 1)
    head_lanes = (lane < SB_HEAD_DIM, lane >= SB_HEAD_DIM)
    tri_row = lax.broadcasted_iota(jnp.int32, (2 * n, 2 * n), 0) & (n - 1)
    tri_col = lax.broadcasted_iota(jnp.int32, (2 * n, 2 * n), 1)
    tri_ones = -jnp.logical_or(tri_col >= n, tri_row > tri_col).astype(BF16)
    key_s = lax.broadcasted_iota(jnp.int32, (1, n), 1)
    row_t = lax.broadcasted_iota(jnp.int32, (2 * qb, 1), 0) & (qb - 1)
    causal = key_s < row_t

    def lanes(p):
        return slice(p * LANES, (p + 1) * LANES)

    def per_pair(ref, rows):
        return [ref[0, rows, lanes(p)].astype(BF16) for p in range(pairs)]

    def per_pair_t(ref, cols):
        return [ref[lanes(p), cols].astype(BF16) for p in range(pairs)]

    def stacked_q(rows=slice(None)):
        q2 = []
        for p in range(pairs):
            qp = q_ref[0, rows, lanes(p)]
            q2.append(jnp.concatenate([jnp.where(m, qp, jnp.zeros_like(qp)) for m in head_lanes], axis=0))
        return q2

    def first_block():
        own = pl.ds(pl.multiple_of(i * qb, qb), qb)
        ks, vs = per_pair(kn_ref, own), per_pair(vn_ref, own)
        if qb < n:
            pad = jnp.zeros((n - qb, LANES), BF16)
            ks = [jnp.concatenate([x, pad], axis=0) for x in ks]
            vs = [jnp.concatenate([x, pad], axis=0) for x in vs]
        return ks, vs, causal, False

    def next_block(d):
        if n_cache:
            cols = slice((2 - d) * n, (3 - d) * n)
            return per_pair_t(kwin_ref, cols), per_pair_t(vwin_ref, cols), None, True
        rows = pl.ds(pl.multiple_of((i - d) * n, n), n)
        return per_pair(kn_ref, rows), per_pair(vn_ref, rows), None, False

    def older_block(start, mask):
        if n_cache:
            cols = pl.ds(start, n)
            copies = [pltpu.make_async_copy(kc_hbm.at[layer, bi, :, cols], kold_s, old_sem.at[0]),
                      pltpu.make_async_copy(vc_hbm.at[layer, bi, :, cols], vold_s, old_sem.at[1])]
            for cp in copies:
                cp.start()
            for cp in copies:
                cp.wait()
            return per_pair_t(kold_s, slice(None)), per_pair_t(vold_s, slice(None)), mask, True
        rows = pl.ds(start, n)
        return per_pair(kn_ref, rows), per_pair(vn_ref, rows), mask, False

    def run(blocks, fresh_start):
        if fresh_start:
            c = [jnp.zeros((2 * qb, n), F32)] * pairs
            acc = [jnp.zeros((qb, LANES), F32)] * pairs
        else:
            c = [c_s[p] for p in range(pairs)]
            acc = [acc_s[p] for p in range(pairs)]
        ((c, acc),) = _sb_blocks([(stacked_q(), blocks, c, acc)], tri_ones, head_lanes)
        for p in range(pairs):
            c_s[p], acc_s[p] = c[p], acc[p]

    half = qb // 2

    def two_halves():
        in_reach = key_s < (row_t & (half - 1))[:2 * half] + (n - half)
        chains = []
        for u in range(2):
            first = pl.multiple_of(i * qb + (u + 1) * half - n, half)
            before = pl.multiple_of(first - n, half)
            blocks = [(per_pair(kn_ref, pl.ds(first, n)), per_pair(vn_ref, pl.ds(first, n)), in_reach, False),
                      (per_pair(kn_ref, pl.ds(before, n)), per_pair(vn_ref, pl.ds(before, n)), None, False)]
            chains.append((stacked_q(slice(u * half, (u + 1) * half)), blocks,
                           [jnp.zeros((2 * half, n), F32)] * pairs, [jnp.zeros((half, LANES), F32)] * pairs))
        for u, (c, acc) in enumerate(_sb_blocks(chains, tri_ones, head_lanes)):
            for p in range(pairs):
                for head in range(2):
                    c_s[p, head * qb + u * half:head * qb + (u + 1) * half, :] = c[p][head * half:(head + 1) * half]
                acc_s[p, u * half:(u + 1) * half, :] = acc[p]

    if n_cache:
        run([first_block(), next_block(1), next_block(2)], True)
        limit_rows = jnp.full((2 * qb, 1), n_cache - 2 * n, jnp.int32)
        unvisited = jnp.int32(n_cache - 2 * n)
    else:
        pl.when(i >= 2)(two_halves)
        pl.when(i < 2)(lambda: run([first_block()], True))
        second_half = (row_t >= half).astype(jnp.int32)
        limit_rows = jnp.where(i >= 2, i * qb + (second_half + 1) * half - 2 * n, i * qb)
        unvisited = jnp.where(i >= 2, i * qb + 2 * half - 2 * n, i * qb)

    def live(rem):
        return jnp.logical_and(rem > 0, jnp.max(c_s[...]) >= SB_LOG_ZERO)

    def one_block(rem):
        start = pl.multiple_of(rem - n, n)
        run([older_block(start, start + key_s < limit_rows)], False)
        return start

    lax.while_loop(live, one_block, unvisited)
    o_ref[0] = jnp.concatenate([acc_s[p] for p in range(pairs)], axis=1).astype(o_ref.dtype)


def _sb_attention(q, k_new, v_new, k_cache, v_cache, layer):
    b, t, _ = q.shape
    n_cache = 0 if k_cache is None else k_cache.shape[3]
    qb = min(SB_QUERY_BLOCK, t)
    assert t % qb == 0 and (n_cache == 0 or t == qb)
    n = SB_KEY_BLOCK
    pairs = SB_WIDTH // LANES
    blk = lambda bi, i: (bi, i, 0)
    full = lambda bi, i: (bi, 0, 0)
    in_specs = [pl.BlockSpec((1, qb, SB_WIDTH), blk),
                pl.BlockSpec((1, t, SB_WIDTH), full),
                pl.BlockSpec((1, t, SB_WIDTH), full)]
    args = [q, k_new, v_new]
    scratch = [pltpu.VMEM((pairs, qb, LANES), F32), pltpu.VMEM((pairs, 2 * qb, n), F32)]
    if n_cache:
        newest = lambda bi, i: (layer, bi, 0, n_cache // (2 * n) - 1)
        in_specs += [pl.BlockSpec((None, None, SB_WIDTH, 2 * n), newest)] * 2 + [pl.BlockSpec(memory_space=pl.ANY)] * 2
        args += [k_cache, v_cache, k_cache, v_cache]
        scratch += [pltpu.VMEM((SB_WIDTH, n), F32), pltpu.VMEM((SB_WIDTH, n), F32), pltpu.SemaphoreType.DMA((2,))]
    return pl.pallas_call(
        functools.partial(_sb_kernel, qb=qb, n_cache=n_cache, layer=layer),
        grid=(b, t // qb),
        in_specs=in_specs,
        out_specs=pl.BlockSpec((1, qb, SB_WIDTH), blk),
        out_shape=jax.ShapeDtypeStruct((b, t, SB_WIDTH), BF16),
        scratch_shapes=scratch,
        compiler_params=_params("parallel", "arbitrary"),
        name="sb_attention",
    )(*args)


def _gla_tile(q, k, v, g, st0, masks, chunks_per_seq):
    tril, causal, k_heads, v_heads, st_heads = masks
    m = q.shape[0]
    c = GLA_CHUNK
    g_hi = g.astype(BF16)
    g_lo = (g - g_hi.astype(F32)).astype(BF16)
    bs, mids, lasts = [], [], []
    for ci in range(m // c):
        r = slice(ci * c, (ci + 1) * c)
        b_c = _dot(tril, g_hi[r]) + _dot(tril, g_lo[r])
        bs.append(b_c)
        mids.append(jnp.broadcast_to(b_c[c // 2 - 1:c // 2], b_c.shape))
        lasts.append(jnp.broadcast_to(b_c[c - 1:c], b_c.shape))
    b, mid, last = (jnp.concatenate(x, axis=0) for x in (bs, mids, lasts))
    q_in = (q * jnp.exp(b - mid)).astype(BF16)
    k_in = k * jnp.exp(mid - b)
    q_dec = (q * jnp.exp(b)).astype(BF16)
    k_out = (k * jnp.exp(last - b)).astype(BF16)
    decay = jnp.exp(last)
    v_bf = v.astype(BF16)
    intra, updates = [], []
    for ci in range(m // c):
        r = slice(ci * c, (ci + 1) * c)
        k_rows = (jnp.concatenate([k_in[r]] * GLA_HEADS, axis=0) * k_heads).astype(BF16)
        v_rows = jnp.concatenate([v_bf[r]] * GLA_HEADS, axis=0) * v_heads
        scores = jnp.where(causal, _dot_nt(q_in[r], k_rows), 0.0).astype(BF16)
        intra.append(_dot(scores, v_rows))
        updates.append(_dot_tn(v_bf[r], k_out[r]) * st_heads)
    o_parts, st_new = [], []
    for s, st in enumerate(st0):
        for j in range(chunks_per_seq):
            ci = s * chunks_per_seq + j
            r = slice(ci * c, (ci + 1) * c)
            o_parts.append(_dot_nt(q_dec[r], st.astype(BF16)) + intra[ci])
            st = st * decay[ci * c:ci * c + 1] + updates[ci]
        st_new.append(st)
    return jnp.concatenate(o_parts, axis=0), st_new


def _mixer_kernel(x_ref, ya_ref, st0_ref, hist0_ref, norm_ref, wp_ref, wgate_ref, glawg_ref, glabg_ref, glanorm_ref,
                  poolw_ref, poolscale_ref, wa_ref, wb_ref, wc_ref, wout_ref,
                  o_ref, st_ref, hist_ref, st_s, ext_s, *, seqs, rows, n_past):
    t = pl.program_id(1)
    m = seqs * rows

    @pl.when(t == 0)
    def _():
        st_s[...] = st0_ref[...]
        ext_s[:, 0:POOL_HIST_ROWS, :] = hist0_ref[...]

    x = x_ref[...].reshape(m, D_MODEL)
    h = _rms(x, norm_ref[...]).astype(BF16)
    u = _dot(h, wp_ref[...])
    kw, vw = GLA_KEY_WIDTH, GLA_VAL_WIDTH
    q_b = u[:, 0:kw] * GLA_KEY_DIM ** -0.5
    k_b = u[:, kw:2 * kw]
    v_b = u[:, 2 * kw:2 * kw + vw]
    r_b = u[:, 2 * kw + vw:3 * kw + vw]
    o_b = u[:, 3 * kw + vw:3 * kw + 2 * vw]
    u_c = u[:, 3 * kw + 2 * vw:]

    gate = _dot(r_b.astype(BF16), glawg_ref[...]) + glabg_ref[...]
    log_alpha = (jnp.minimum(gate, 0.0) - jnp.log1p(jnp.exp(-jnp.abs(gate)))) / GLA_GATE_TAU
    c = GLA_CHUNK
    ri = lax.broadcasted_iota(jnp.int32, (c, c), 0)
    ci = lax.broadcasted_iota(jnp.int32, (c, c), 1)
    tril = (ci <= ri).astype(BF16)
    key_head = _iota_div((1, kw), 1, GLA_KEY_DIM)
    val_head = _iota_div((1, vw), 1, GLA_VAL_DIM)
    rows_hs = _iota_div((GLA_HEADS * c, 1), 0, c)
    causal = (lax.broadcasted_iota(jnp.int32, (c, GLA_HEADS * c), 1) & (c - 1)) <= lax.broadcasted_iota(
        jnp.int32, (c, GLA_HEADS * c), 0)
    k_heads = (rows_hs == key_head).astype(F32)
    v_heads = (rows_hs == val_head).astype(BF16)
    st_heads = (_iota_div((vw, 1), 0, GLA_VAL_DIM) == key_head).astype(F32)
    masks = (tril, causal, k_heads, v_heads, st_heads)
    o, st_new = _gla_tile(q_b, k_b, v_b, log_alpha, [st_s[s] for s in range(seqs)], masks, rows // c)
    for s in range(seqs):
        st_s[s] = st_new[s]
    head_mean = (_iota_div((vw, 1), 0, GLA_VAL_DIM) == val_head).astype(BF16) * (1.0 / GLA_VAL_DIM)
    ms = _dot_split(o * o, head_mean)
    y_b = o * lax.rsqrt(ms + RMS_EPS) * glanorm_ref[...] * (o_b * jax.nn.sigmoid(o_b))

    lane = lax.broadcasted_iota(jnp.int32, (1, POOL_WIDTH), 1)
    pos1 = n_past + t * rows + 1 + lax.broadcasted_iota(jnp.int32, (rows, 1), 0)
    d_parts = []
    for s in range(seqs):
        ext_s[s, POOL_HIST_ROWS:, :] = u_c[s * rows:(s + 1) * rows]
        e = ext_s[s]
        sums = []
        span = 1
        for w in POOL_WINDOWS:
            while span < w:
                e = e + pltpu.roll(e, span, 0)
                span *= 2
            sums.append(e[POOL_HIST_ROWS:])
        wsum = sums[-1]
        cnt = jnp.minimum(POOL_WINDOWS[-1], pos1)
        for gi in range(len(POOL_WINDOWS) - 2, -1, -1):
            in_group = lane < (gi + 1) * POOL_GROUP_DIM
            wsum = jnp.where(in_group, sums[gi], wsum)
            cnt = jnp.where(in_group, jnp.minimum(POOL_WINDOWS[gi], pos1), cnt)
        d_parts.append(wsum / cnt.astype(F32) - u_c[s * rows:(s + 1) * rows])
        ext_s[s, 0:POOL_HIST_ROWS, :] = ext_s[s, rows:rows + POOL_HIST_ROWS, :]
    d = jnp.concatenate(d_parts, axis=0)
    y_c = _dot(d.astype(BF16), poolw_ref[...]) * poolscale_ref[...]

    branches = ((ya_ref[...].reshape(m, SB_WIDTH), wa_ref), (y_b.astype(BF16), wb_ref), (y_c.astype(BF16), wc_ref))
    merged = None
    for bi, (y, w_ref) in enumerate(branches):
        gate_b = jax.nn.sigmoid(_dot(h, wgate_ref[:, bi * D_MODEL:(bi + 1) * D_MODEL]))
        term = gate_b * _dot(y, w_ref[...])
        merged = term if merged is None else merged + term
    o_ref[...] = (x + _dot(merged.astype(BF16), wout_ref[...])).reshape(seqs, rows, D_MODEL)

    @pl.when(t == pl.num_programs(1) - 1)
    def _():
        st_ref[...] = st_s[...]
        hist_ref[...] = ext_s[:, 0:POOL_HIST_ROWS, :]


def _mixer(x, y_a, st0, hist0, w, n_past):
    b, t, _ = x.shape
    rows = min(TOKEN_TILE, t)
    seqs = TOKEN_TILE // rows
    assert t % rows == 0 and b % seqs == 0 and rows % GLA_CHUNK == 0
    tok = lambda bi, ti: (bi, ti, 0)
    per_seq = lambda bi, ti: (bi, 0, 0)
    kw, vw = GLA_KEY_WIDTH, GLA_VAL_WIDTH
    return pl.pallas_call(
        functools.partial(_mixer_kernel, seqs=seqs, rows=rows, n_past=n_past),
        grid=(b // seqs, t // rows),
        in_specs=[pl.BlockSpec((seqs, rows, D_MODEL), tok),
                  pl.BlockSpec((seqs, rows, SB_WIDTH), tok),
                  pl.BlockSpec((seqs, vw, kw), per_seq),
                  pl.BlockSpec((seqs, POOL_HIST_ROWS, POOL_WIDTH), per_seq),
                  _const_spec((1, D_MODEL)),
                  _const_spec((D_MODEL, 3 * kw + 3 * vw)),
                  _const_spec((D_MODEL, 3 * D_MODEL)),
                  _const_spec((kw, kw)),
                  _const_spec((1, kw)),
                  _const_spec((1, vw)),
                  _const_spec((POOL_WIDTH, POOL_WIDTH)),
                  _const_spec((1, POOL_WIDTH)),
                  _const_spec((SB_WIDTH, D_MODEL)),
                  _const_spec((vw, D_MODEL)),
                  _const_spec((POOL_WIDTH, D_MODEL)),
                  _const_spec((D_MODEL, D_MODEL))],
        out_specs=[pl.BlockSpec((seqs, rows, D_MODEL), tok),
                   pl.BlockSpec((seqs, vw, kw), per_seq),
                   pl.BlockSpec((seqs, POOL_HIST_ROWS, POOL_WIDTH), per_seq)],
        out_shape=[jax.ShapeDtypeStruct((b, t, D_MODEL), F32),
                   jax.ShapeDtypeStruct((b, vw, kw), F32),
                   jax.ShapeDtypeStruct((b, POOL_HIST_ROWS, POOL_WIDTH), F32)],
        scratch_shapes=[pltpu.VMEM((seqs, vw, kw), F32),
                        pltpu.VMEM((seqs, POOL_HIST_ROWS + rows, POOL_WIDTH), F32)],
        compiler_params=_params("parallel", "arbitrary"),
        name="mixer",
    )(x, y_a, st0, hist0, w["mix_norm"], w["w_proj"], w["w_gate"], w["gla_wg"], w["gla_bg"], w["gla_norm"],
      w["pool_w"], w["pool_scale"], w["w_branch_a"], w["w_branch_b"], w["w_branch_c"], w["w_out"])


def _layer_weights(i, p):
    row = lambda a: a[i].reshape(1, -1).astype(F32)
    bf = lambda a: a.astype(BF16)
    kw, vw = GLA_KEY_WIDTH, GLA_VAL_WIDTH
    w_in = p["w_in"][i]
    a_end = 3 * SB_WIDTH
    qkv_b = w_in[:, a_end:a_end + 2 * kw + vw]
    r0 = a_end + 2 * kw + vw
    r_b = jnp.pad(w_in[:, r0:r0 + GLA_GATE_RANK], ((0, 0), (0, kw - GLA_GATE_RANK)))
    o0 = r0 + GLA_GATE_RANK
    rest = w_in[:, o0:o0 + vw + POOL_WIDTH]
    g0 = o0 + vw + POOL_WIDTH
    pool_w = jnp.zeros((POOL_WIDTH, POOL_WIDTH), F32)
    for gi in range(len(POOL_WINDOWS)):
        lo = gi * POOL_GROUP_DIM
        pool_w = pool_w.at[lo:lo + POOL_GROUP_DIM, lo:lo + POOL_GROUP_DIM].set(p["pool_w"][i, gi])
    return dict(
        ffn1_norm=row(p["ffn1_norm"]), ffn1_wa=bf(p["ffn1_w_in"][i, :, :FFN_DIM]), ffn1_wb=bf(p["ffn1_w_in"][i, :, FFN_DIM:]),
        ffn1_wo=bf(p["ffn1_w_out"][i]),
        mix_norm=row(p["mix_norm"]), w_qkv=bf(w_in[:, :a_end]),
        w_proj=bf(jnp.concatenate([qkv_b, r_b, rest], axis=1)), w_gate=bf(w_in[:, g0:]),
        gla_wg=bf(jnp.pad(p["gla_w_gate"][i], ((0, kw - GLA_GATE_RANK), (0, 0)))),
        gla_bg=row(p["gla_b_gate"]), gla_norm=row(p["gla_norm"]),
        pool_w=bf(pool_w), pool_scale=row(p["pool_scale"]),
        w_branch_a=bf(p["w_branch_a"][i]), w_branch_b=bf(p["w_branch_b"][i]), w_branch_c=bf(p["w_branch_c"][i]),
        w_out=bf(p["w_out"][i]),
        ffn2_norm=row(p["ffn2_norm"]), ffn2_wa=bf(p["ffn2_w_in"][i, :, :FFN_DIM]), ffn2_wb=bf(p["ffn2_w_in"][i, :, FFN_DIM:]),
        ffn2_wo=bf(p["ffn2_w_out"][i]),
        ple_norm=row(p["ple_norm"]), ple_wg=bf(p["ple_w_gate"][i]), ple_wp=bf(p["ple_w_proj"][i]),
    )


def _state_to_blocks(s):
    b = s.shape[0]
    eye = jnp.eye(GLA_HEADS, dtype=s.dtype)
    blocks = jnp.einsum("bhde,hg->bhegd", s, eye)
    return blocks.reshape(b, GLA_VAL_WIDTH, GLA_KEY_WIDTH)


def _blocks_to_state(st):
    b = st.shape[0]
    blocks = st.reshape(b, GLA_HEADS, GLA_VAL_DIM, GLA_HEADS, GLA_KEY_DIM)
    diag = jnp.stack([blocks[:, h, :, h, :] for h in range(GLA_HEADS)], axis=1)
    return diag.transpose(0, 1, 3, 2)


def _run_group(x, p, k_cache, v_cache, s_gla, s_pool, layers, final_norm):
    b, t, _ = x.shape
    n = b * t
    depth = len(layers)
    has_cache = k_cache is not None
    n_past = k_cache.shape[2] if has_cache else 0
    if has_cache:
        k_cache, v_cache = (a.transpose(0, 1, 3, 4, 2).reshape(depth, b, SB_WIDTH, n_past) for a in (k_cache, v_cache))
    x = x.reshape(n, D_MODEL)
    k_all = v_all = None
    ss, ps = [], []
    for i, w in enumerate(layers):
        x, q, k, v, k_all, v_all = _ffn_qkv(x, w, k_all, v_all, i, depth)
        q, k3, v3 = (a.reshape(b, t, SB_WIDTH) for a in (q, k, v))
        if has_cache:
            st0 = _state_to_blocks(s_gla[i])
            hist0 = jnp.pad(s_pool[i], ((0, 0), (POOL_HIST_ROWS - POOL_HIST, 0), (0, 0)))
        else:
            st0 = jnp.zeros((b, GLA_VAL_WIDTH, GLA_KEY_WIDTH), F32)
            hist0 = jnp.zeros((b, POOL_HIST_ROWS, POOL_WIDTH), F32)
        y_a = _sb_attention(q, k3, v3, k_cache, v_cache, i)
        x3, st, hist = _mixer(x.reshape(b, t, D_MODEL), y_a, st0, hist0, w, n_past)
        x = _ffn_ple(x3.reshape(n, D_MODEL), p.reshape(depth, n, PLE_DIM), i, w, final_norm, final=(i == depth - 1))
        ss.append(_blocks_to_state(st))
        ps.append(hist[:, POOL_HIST_ROWS - POOL_HIST:, :])
    heads = (depth, b, t, SB_HEADS, SB_HEAD_DIM)
    return x.reshape(b, t, D_MODEL), k_all.reshape(heads), v_all.reshape(heads), jnp.stack(ss), jnp.stack(ps)


def kernel(x_prompt, x_sample, cache_sb_k, cache_sb_v, state_gla, state_pool, p_prompt, p_sample, ffn1_norm, ffn1_w_in, ffn1_w_out, mix_norm, w_in, gla_w_gate, gla_b_gate, gla_norm, pool_w, pool_scale, w_branch_a, w_branch_b, w_branch_c, w_out, ffn2_norm, ffn2_w_in, ffn2_w_out, ple_norm, ple_w_gate, ple_w_proj, final_norm):
    params = dict(ffn1_norm=ffn1_norm, ffn1_w_in=ffn1_w_in, ffn1_w_out=ffn1_w_out, mix_norm=mix_norm, w_in=w_in,
                  gla_w_gate=gla_w_gate, gla_b_gate=gla_b_gate, gla_norm=gla_norm, pool_w=pool_w, pool_scale=pool_scale,
                  w_branch_a=w_branch_a, w_branch_b=w_branch_b, w_branch_c=w_branch_c, w_out=w_out,
                  ffn2_norm=ffn2_norm, ffn2_w_in=ffn2_w_in, ffn2_w_out=ffn2_w_out, ple_norm=ple_norm,
                  ple_w_gate=ple_w_gate, ple_w_proj=ple_w_proj)
    depth = w_in.shape[0]
    layers = [_layer_weights(i, params) for i in range(depth)]
    fnorm = final_norm.reshape(1, D_MODEL).astype(F32)
    prompt = _run_group(x_prompt, p_prompt, None, None, None, None, layers, fnorm)
    sample = _run_group(x_sample, p_sample, cache_sb_k, cache_sb_v, state_gla, state_pool, layers, fnorm)
    return (prompt[0], sample[0], prompt[1], prompt[2], prompt[3], prompt[4],
            sample[1], sample[2], sample[3], sample[4])
```

```python
import functools

import jax
import jax.numpy as jnp
from jax import lax
from jax.experimental import pallas as pl
from jax.experimental.pallas import tpu as pltpu

D_MODEL = 1024
FFN_DIM = 2816
PLE_DIM = 256
SB_HEADS = 8
SB_HEAD_DIM = 64
SB_WIDTH = SB_HEADS * SB_HEAD_DIM
GLA_HEADS = 4
GLA_KEY_DIM = 32
GLA_VAL_DIM = 64
GLA_KEY_WIDTH = GLA_HEADS * GLA_KEY_DIM
GLA_VAL_WIDTH = GLA_HEADS * GLA_VAL_DIM
GLA_GATE_RANK = 16
GLA_GATE_TAU = 16.0
GLA_CHUNK = 64
POOL_WINDOWS = (2, 4, 8, 16)
POOL_GROUP_DIM = 64
POOL_WIDTH = len(POOL_WINDOWS) * POOL_GROUP_DIM
POOL_HIST = 15
POOL_HIST_ROWS = 16
RMS_EPS = 1e-6

LANES = 128
VMEM_LIMIT_BYTES = 56 * 1024 * 1024
TOKEN_TILE = 512
MIXER_TILE = 1024
FFN_CHUNK = 256
SB_QUERY_BLOCK = 128
SB_KEY_BLOCK = 128
SB_LOG_ZERO = -104.0

F32 = jnp.float32
BF16 = jnp.bfloat16


def _dot(a, b):
    return jnp.dot(a, b, preferred_element_type=F32)


def _dot_nt(a, b):
    return lax.dot_general(a, b, (((1,), (1,)), ((), ())), preferred_element_type=F32)


def _dot_tn(a, b):
    return lax.dot_general(a, b, (((0,), (0,)), ((), ())), preferred_element_type=F32)


def _dot_split(a, b):
    hi = a.astype(BF16)
    lo = (a - hi.astype(F32)).astype(BF16)
    return _dot(hi, b) + _dot(lo, b)


def _iota_div(shape, axis, size):
    assert size & (size - 1) == 0
    return lax.broadcasted_iota(jnp.int32, shape, axis) >> (size.bit_length() - 1)


def _rms(x, g):
    ms = jnp.mean(x * x, axis=-1, keepdims=True)
    return x * lax.rsqrt(ms + RMS_EPS) * g


def _params(*semantics):
    return pltpu.CompilerParams(dimension_semantics=semantics, vmem_limit_bytes=VMEM_LIMIT_BYTES)


def _const_spec(shape):
    zeros = (0,) * len(shape)
    return pl.BlockSpec(shape, lambda *_: zeros, pipeline_mode=pl.Buffered(1))


def _swiglu_update(x, norm_ref, wa_ref, wb_ref, wo_ref):
    h = _rms(x, norm_ref[...]).astype(BF16)
    acc = jnp.zeros(x.shape, F32)
    for f in range(0, FFN_DIM, FFN_CHUNK):
        a = _dot(h, wa_ref[:, f:f + FFN_CHUNK])
        b = _dot(h, wb_ref[:, f:f + FFN_CHUNK])
        g = (a * jax.nn.sigmoid(a) * b).astype(BF16)
        acc = acc + _dot(g, wo_ref[f:f + FFN_CHUNK, :])
    return x + 0.5 * acc


def _ffn_ple_kernel(x_ref, p_ref, norm_ref, wa_ref, wb_ref, wo_ref, pnorm_ref, wpg_ref, wpp_ref, fnorm_ref,
                    o_ref, *, final):
    x = _swiglu_update(x_ref[...], norm_ref, wa_ref, wb_ref, wo_ref)
    gate = jax.nn.sigmoid(_dot(_rms(x, pnorm_ref[...]).astype(BF16), wpg_ref[...]))
    x = x + gate * _dot(p_ref[...].astype(BF16), wpp_ref[...])
    if final:
        x = _rms(x, fnorm_ref[...])
    o_ref[...] = x


def _ffn_ple(x, p, layer, w, final_norm, final):
    n = x.shape[0]
    tok = lambda i: (i, 0)
    return pl.pallas_call(
        functools.partial(_ffn_ple_kernel, final=final),
        grid=(n // TOKEN_TILE,),
        in_specs=[pl.BlockSpec((TOKEN_TILE, D_MODEL), tok),
                  pl.BlockSpec((None, TOKEN_TILE, PLE_DIM), lambda i: (layer, i, 0)),
                  _const_spec((1, D_MODEL)),
                  _const_spec((D_MODEL, FFN_DIM)),
                  _const_spec((D_MODEL, FFN_DIM)),
                  _const_spec((FFN_DIM, D_MODEL)),
                  _const_spec((1, D_MODEL)),
                  _const_spec((D_MODEL, D_MODEL)),
                  _const_spec((PLE_DIM, D_MODEL)),
                  _const_spec((1, D_MODEL))],
        out_specs=pl.BlockSpec((TOKEN_TILE, D_MODEL), tok),
        out_shape=jax.ShapeDtypeStruct((n, D_MODEL), F32),
        compiler_params=_params("parallel"),
        name="ffn_ple",
    )(x, p, w["ffn2_norm"], w["ffn2_wa"], w["ffn2_wb"], w["ffn2_wo"], w["ple_norm"], w["ple_wg"], w["ple_wp"],
      final_norm)


def _ffn_qkv_kernel(*refs, layer, first):
    if first:
        x_ref, norm_ref, wa_ref, wb_ref, wo_ref, mnorm_ref, w_ref, o_ref, q_ref, kb_ref, vb_ref, k_ref, v_ref = refs
        for ref in (k_ref, v_ref):
            for other in range(ref.shape[0]):
                if other != layer:
                    ref[other] = jnp.zeros(ref.shape[1:], ref.dtype)
        k_ref, v_ref = k_ref.at[layer], v_ref.at[layer]
    else:
        (x_ref, norm_ref, wa_ref, wb_ref, wo_ref, mnorm_ref, w_ref, _, _,
         o_ref, q_ref, kb_ref, vb_ref, k_ref, v_ref) = refs
    x = _swiglu_update(x_ref[...], norm_ref, wa_ref, wb_ref, wo_ref)
    o_ref[...] = x
    u = _dot(_rms(x, mnorm_ref[...]).astype(BF16), w_ref[...])
    q_ref[...] = (u[:, :SB_WIDTH] * SB_HEAD_DIM ** -0.5).astype(BF16)
    k = u[:, SB_WIDTH:2 * SB_WIDTH]
    v = u[:, 2 * SB_WIDTH:]
    for head in range(SB_HEADS):
        cols = slice(head * SB_HEAD_DIM, (head + 1) * SB_HEAD_DIM)
        k_ref[pl.ds(head, TOKEN_TILE, stride=SB_HEADS), :] = k[:, cols]
        v_ref[pl.ds(head, TOKEN_TILE, stride=SB_HEADS), :] = v[:, cols]
    kb_ref[...] = k.astype(BF16)
    vb_ref[...] = v.astype(BF16)


def _ffn_qkv(x, w, k_all, v_all, layer, depth):
    n = x.shape[0]
    first = k_all is None
    tok = lambda i: (i, 0)
    kv_rows = TOKEN_TILE * SB_HEADS
    kv_shape = jax.ShapeDtypeStruct((depth, n * SB_HEADS, SB_HEAD_DIM), F32)
    if first:
        kv_spec = pl.BlockSpec((depth, kv_rows, SB_HEAD_DIM), lambda i: (0, i, 0))
        extra_specs, extra_args, aliases = [], [], {}
    else:
        kv_spec = pl.BlockSpec((None, kv_rows, SB_HEAD_DIM), lambda i: (layer, i, 0))
        extra_specs, extra_args, aliases = [pl.BlockSpec(memory_space=pl.ANY)] * 2, [k_all, v_all], {7: 4, 8: 5}
    return pl.pallas_call(
        functools.partial(_ffn_qkv_kernel, layer=layer, first=first),
        grid=(n // TOKEN_TILE,),
        in_specs=[pl.BlockSpec((TOKEN_TILE, D_MODEL), tok),
                  _const_spec((1, D_MODEL)),
                  _const_spec((D_MODEL, FFN_DIM)),
                  _const_spec((D_MODEL, FFN_DIM)),
                  _const_spec((FFN_DIM, D_MODEL)),
                  _const_spec((1, D_MODEL)),
                  _const_spec((D_MODEL, 3 * SB_WIDTH))] + extra_specs,
        out_specs=[pl.BlockSpec((TOKEN_TILE, D_MODEL), tok)] + [pl.BlockSpec((TOKEN_TILE, SB_WIDTH), tok)] * 3
        + [kv_spec] * 2,
        out_shape=[jax.ShapeDtypeStruct((n, D_MODEL), F32)] + [jax.ShapeDtypeStruct((n, SB_WIDTH), BF16)] * 3
        + [kv_shape] * 2,
        input_output_aliases=aliases,
        compiler_params=_params("parallel"),
        name="ffn_qkv",
    )(x, w["ffn1_norm"], w["ffn1_wa"], w["ffn1_wb"], w["ffn1_wo"], w["mix_norm"], w["w_qkv"], *extra_args)


def _sb_blocks(chains, tri_ones, head_lanes):
    n = tri_ones.shape[0] // 2
    row = lax.broadcasted_iota(jnp.int32, (LANES, 1), 0)
    head_rows = (row < SB_HEAD_DIM, row >= SB_HEAD_DIM)
    log_beta, split = [], []
    for q2, blocks, _, _ in chains:
        for k, _, mask, transposed in blocks:
            for p in range(len(q2)):
                z = _dot(q2[p], k[p]) if transposed else _dot_nt(q2[p], k[p])
                sp = jnp.maximum(z, 0.0) + jnp.log(1.0 + jnp.exp(-jnp.abs(z)))
                log_beta.append(z - sp)
                if mask is not None:
                    sp = jnp.where(mask, sp, 0.0)
                hi = sp.astype(BF16)
                lo = (sp - hi.astype(F32)).astype(BF16)
                split.append(jnp.concatenate([hi, lo], axis=1))
    x = _dot(jnp.concatenate(split, axis=0), tri_ones)
    results = []
    visit = 0
    r0 = 0
    for q2, blocks, c, acc in chains:
        c, acc = list(c), list(acc)
        rows = acc[0].shape[0]
        for _, v, mask, transposed in blocks:
            for p in range(len(q2)):
                xs = x[r0:r0 + 2 * rows]
                w = jnp.exp(log_beta[visit] + xs[:, :n] + c[p])
                if mask is not None:
                    w = jnp.where(mask, w, 0.0)
                c[p] = c[p] + xs[:, n:]
                w2 = jnp.concatenate([w[:rows], w[rows:]], axis=1).astype(BF16)
                zero = jnp.zeros_like(v[p])
                if transposed:
                    v2 = jnp.concatenate([jnp.where(m, v[p], zero) for m in head_rows], axis=1)
                    acc[p] = acc[p] + _dot_nt(w2, v2)
                else:
                    v2 = jnp.concatenate([jnp.where(m, v[p], zero) for m in head_lanes], axis=0)
                    acc[p] = acc[p] + _dot(w2, v2)
                visit += 1
                r0 += 2 * rows
        results.append((c, acc))
    return results


def _sb_kernel(*refs, qb, n_cache, layer):
    n = SB_KEY_BLOCK
    pairs = SB_WIDTH // LANES
    if n_cache:
        (q_ref, kn_ref, vn_ref, kwin_ref, vwin_ref, kc_hbm, vc_hbm, o_ref, acc_s, c_s, kold_s, vold_s,
         old_sem) = refs
        assert n_cache % n == 0 and n_cache >= 2 * n and kn_ref.shape[1] == qb
    else:
        q_ref, kn_ref, vn_ref, o_ref, acc_s, c_s = refs
        assert qb == n
    bi = pl.program_id(0)
    i = pl.program_id(1)

    lane = lax.broadcasted_iota(jnp.int32, (1, LANES), 1)
    head_lanes = (lane < SB_HEAD_DIM, lane >= SB_HEAD_DIM)
    tri_row = lax.broadcasted_iota(jnp.int32, (2 * n, 2 * n), 0) & (n - 1)
    tri_col = lax.broadcasted_iota(jnp.int32, (2 * n, 2 * n), 1)
    tri_ones = -jnp.logical_or(tri_col >= n, tri_row > tri_col).astype(BF16)
    key_s = lax.broadcasted_iota(jnp.int32, (1, n), 1)
    row_t = lax.broadcasted_iota(jnp.int32, (2 * qb, 1), 0) & (qb - 1)
    causal = key_s < row_t

    def lanes(p):
        return slice(p * LANES, (p + 1) * LANES)

    def per_pair(ref, rows):
        return [ref[0, rows, lanes(p)].astype(BF16) for p in range(pairs)]

    def per_pair_t(ref, cols):
        return [ref[lanes(p), cols].astype(BF16) for p in range(pairs)]

    def stacked_q(rows=slice(None)):
        q2 = []
        for p in range(pairs):
            qp = q_ref[0, rows, lanes(p)]
            q2.append(jnp.concatenate([jnp.where(m, qp, jnp.zeros_like(qp)) for m in head_lanes], axis=0))
        return q2

    def first_block():
        own = pl.ds(pl.multiple_of(i * qb, qb), qb)
        ks, vs = per_pair(kn_ref, own), per_pair(vn_ref, own)
        if qb < n:
            pad = jnp.zeros((n - qb, LANES), BF16)
            ks = [jnp.concatenate([x, pad], axis=0) for x in ks]
            vs = [jnp.concatenate([x, pad], axis=0) for x in vs]
        return ks, vs, causal, False

    def next_block(d):
        if n_cache:
            cols = slice((2 - d) * n, (3 - d) * n)
            return per_pair_t(kwin_ref, cols), per_pair_t(vwin_ref, cols), None, True
        rows = pl.ds(pl.multiple_of((i - d) * n, n), n)
        return per_pair(kn_ref, rows), per_pair(vn_ref, rows), None, False

    def older_block(start, mask):
        if n_cache:
            cols = pl.ds(start, n)
            copies = [pltpu.make_async_copy(kc_hbm.at[layer, bi, :, cols], kold_s, old_sem.at[0]),
                      pltpu.make_async_copy(vc_hbm.at[layer, bi, :, cols], vold_s, old_sem.at[1])]
            for cp in copies:
                cp.start()
            for cp in copies:
                cp.wait()
            return per_pair_t(kold_s, slice(None)), per_pair_t(vold_s, slice(None)), mask, True
        rows = pl.ds(start, n)
        return per_pair(kn_ref, rows), per_pair(vn_ref, rows), mask, False

    def run(blocks, fresh_start):
        if fresh_start:
            c = [jnp.zeros((2 * qb, n), F32)] * pairs
            acc = [jnp.zeros((qb, LANES), F32)] * pairs
        else:
            c = [c_s[p] for p in range(pairs)]
            acc = [acc_s[p] for p in range(pairs)]
        ((c, acc),) = _sb_blocks([(stacked_q(), blocks, c, acc)], tri_ones, head_lanes)
        for p in range(pairs):
            c_s[p], acc_s[p] = c[p], acc[p]

    half = qb // 2

    def two_halves():
        in_reach = key_s < (row_t & (half - 1))[:2 * half] + (n - half)
        chains = []
        for u in range(2):
            first = pl.multiple_of(i * qb + (u + 1) * half - n, half)
            before = pl.multiple_of(first - n, half)
            blocks = [(per_pair(kn_ref, pl.ds(first, n)), per_pair(vn_ref, pl.ds(first, n)), in_reach, False),
                      (per_pair(kn_ref, pl.ds(before, n)), per_pair(vn_ref, pl.ds(before, n)), None, False)]
            chains.append((stacked_q(slice(u * half, (u + 1) * half)), blocks,
                           [jnp.zeros((2 * half, n), F32)] * pairs, [jnp.zeros((half, LANES), F32)] * pairs))
        for u, (c, acc) in enumerate(_sb_blocks(chains, tri_ones, head_lanes)):
            for p in range(pairs):
                for head in range(2):
                    c_s[p, head * qb + u * half:head * qb + (u + 1) * half, :] = c[p][head * half:(head + 1) * half]
                acc_s[p, u * half:(u + 1) * half, :] = acc[p]

    if n_cache:
        run([first_block(), next_block(1), next_block(2)], True)
        limit_rows = jnp.full((2 * qb, 1), n_cache - 2 * n, jnp.int32)
        unvisited = jnp.int32(n_cache - 2 * n)
    else:
        pl.when(i >= 2)(two_halves)
        pl.when(i < 2)(lambda: run([first_block()], True))
        second_half = (row_t >= half).astype(jnp.int32)
        limit_rows = jnp.where(i >= 2, i * qb + (second_half + 1) * half - 2 * n, i * qb)
        unvisited = jnp.where(i >= 2, i * qb + 2 * half - 2 * n, i * qb)

    def live(rem):
        return jnp.logical_and(rem > 0, jnp.max(c_s[...]) >= SB_LOG_ZERO)

    def one_block(rem):
        start = pl.multiple_of(rem - n, n)
        run([older_block(start, start + key_s < limit_rows)], False)
        return start

    lax.while_loop(live, one_block, unvisited)
    o_ref[0] = jnp.concatenate([acc_s[p] for p in range(pairs)], axis=1).astype(o_ref.dtype)


def _sb_attention(q, k_new, v_new, k_cache, v_cache, layer):
    b, t, _ = q.shape
    n_cache = 0 if k_cache is None else k_cache.shape[3]
    qb = min(SB_QUERY_BLOCK, t)
    assert t % qb == 0 and (n_cache == 0 or t == qb)
    n = SB_KEY_BLOCK
    pairs = SB_WIDTH // LANES
    blk = lambda bi, i: (bi, i, 0)
    full = lambda bi, i: (bi, 0, 0)
    in_specs = [pl.BlockSpec((1, qb, SB_WIDTH), blk),
                pl.BlockSpec((1, t, SB_WIDTH), full),
                pl.BlockSpec((1, t, SB_WIDTH), full)]
    args = [q, k_new, v_new]
    scratch = [pltpu.VMEM((pairs, qb, LANES), F32), pltpu.VMEM((pairs, 2 * qb, n), F32)]
    if n_cache:
        newest = lambda bi, i: (layer, bi, 0, n_cache // (2 * n) - 1)
        in_specs += [pl.BlockSpec((None, None, SB_WIDTH, 2 * n), newest)] * 2 + [pl.BlockSpec(memory_space=pl.ANY)] * 2
        args += [k_cache, v_cache, k_cache, v_cache]
        scratch += [pltpu.VMEM((SB_WIDTH, n), F32), pltpu.VMEM((SB_WIDTH, n), F32), pltpu.SemaphoreType.DMA((2,))]
    return pl.pallas_call(
        functools.partial(_sb_kernel, qb=qb, n_cache=n_cache, layer=layer),
        grid=(b, t // qb),
        in_specs=in_specs,
        out_specs=pl.BlockSpec((1, qb, SB_WIDTH), blk),
        out_shape=jax.ShapeDtypeStruct((b, t, SB_WIDTH), BF16),
        scratch_shapes=scratch,
        compiler_params=_params("parallel", "arbitrary"),
        name="sb_attention",
    )(*args)


def _gla_tile(q, k, v, g, st0, masks, chunks_per_seq):
    tril, causal, k_heads, v_heads, st_heads = masks
    m = q.shape[0]
    c = GLA_CHUNK
    g_hi = g.astype(BF16)
    g_lo = (g - g_hi.astype(F32)).astype(BF16)
    bs, mids, lasts = [], [], []
    for ci in range(m // c):
        r = slice(ci * c, (ci + 1) * c)
        b_c = _dot(tril, g_hi[r]) + _dot(tril, g_lo[r])
        bs.append(b_c)
        mids.append(jnp.broadcast_to(b_c[c // 2 - 1:c // 2], b_c.shape))
        lasts.append(jnp.broadcast_to(b_c[c - 1:c], b_c.shape))
    b, mid, last = (jnp.concatenate(x, axis=0) for x in (bs, mids, lasts))
    q_in = (q * jnp.exp(b - mid)).astype(BF16)
    k_in = k * jnp.exp(mid - b)
    q_dec = (q * jnp.exp(b)).astype(BF16)
    k_out = (k * jnp.exp(last - b)).astype(BF16)
    decay = jnp.exp(last)
    v_bf = v.astype(BF16)
    intra, updates = [], []
    for ci in range(m // c):
        r = slice(ci * c, (ci + 1) * c)
        k_rows = (jnp.concatenate([k_in[r]] * GLA_HEADS, axis=0) * k_heads).astype(BF16)
        v_rows = jnp.concatenate([v_bf[r]] * GLA_HEADS, axis=0) * v_heads
        scores = jnp.where(causal, _dot_nt(q_in[r], k_rows), 0.0).astype(BF16)
        intra.append(_dot(scores, v_rows))
        updates.append(_dot_tn(v_bf[r], k_out[r]) * st_heads)
    o_parts, st_new = [], []
    for s, st in enumerate(st0):
        for j in range(chunks_per_seq):
            ci = s * chunks_per_seq + j
            r = slice(ci * c, (ci + 1) * c)
            o_parts.append(_dot_nt(q_dec[r], st.astype(BF16)) + intra[ci])
            st = st * decay[ci * c:ci * c + 1] + updates[ci]
        st_new.append(st)
    return jnp.concatenate(o_parts, axis=0), st_new


def _mixer_kernel(x_ref, ya_ref, st0_ref, hist0_ref, norm_ref, wp_ref, wgate_ref, glawg_ref, glabg_ref, glanorm_ref,
                  poolw_ref, poolscale_ref, wa_ref, wb_ref, wc_ref, wout_ref,
                  o_ref, st_ref, hist_ref, st_s, ext_s, *, seqs, rows, n_past):
    t = pl.program_id(1)
    m = seqs * rows

    @pl.when(t == 0)
    def _():
        st_s[...] = st0_ref[...]
        ext_s[:, 0:POOL_HIST_ROWS, :] = hist0_ref[...]

    x = x_ref[...].reshape(m, D_MODEL)
    h = _rms(x, norm_ref[...]).astype(BF16)
    u = _dot(h, wp_ref[...])
    kw, vw = GLA_KEY_WIDTH, GLA_VAL_WIDTH
    q_b = u[:, 0:kw] * GLA_KEY_DIM ** -0.5
    k_b = u[:, kw:2 * kw]
    v_b = u[:, 2 * kw:2 * kw + vw]
    r_b = u[:, 2 * kw + vw:3 * kw + vw]
    o_b = u[:, 3 * kw + vw:3 * kw + 2 * vw]
    u_c = u[:, 3 * kw + 2 * vw:]

    gate = _dot(r_b.astype(BF16), glawg_ref[...]) + glabg_ref[...]
    log_alpha = (jnp.minimum(gate, 0.0) - jnp.log1p(jnp.exp(-jnp.abs(gate)))) / GLA_GATE_TAU
    c = GLA_CHUNK
    ri = lax.broadcasted_iota(jnp.int32, (c, c), 0)
    ci = lax.broadcasted_iota(jnp.int32, (c, c), 1)
    tril = (ci <= ri).astype(BF16)
    key_head = _iota_div((1, kw), 1, GLA_KEY_DIM)
    val_head = _iota_div((1, vw), 1, GLA_VAL_DIM)
    rows_hs = _iota_div((GLA_HEADS * c, 1), 0, c)
    causal = (lax.broadcasted_iota(jnp.int32, (c, GLA_HEADS * c), 1) & (c - 1)) <= lax.broadcasted_iota(
        jnp.int32, (c, GLA_HEADS * c), 0)
    k_heads = (rows_hs == key_head).astype(F32)
    v_heads = (rows_hs == val_head).astype(BF16)
    st_heads = (_iota_div((vw, 1), 0, GLA_VAL_DIM) == key_head).astype(F32)
    masks = (tril, causal, k_heads, v_heads, st_heads)
    o, st_new = _gla_tile(q_b, k_b, v_b, log_alpha, [st_s[s] for s in range(seqs)], masks, rows // c)
    for s in range(seqs):
        st_s[s] = st_new[s]
    head_mean = (_iota_div((vw, 1), 0, GLA_VAL_DIM) == val_head).astype(BF16) * (1.0 / GLA_VAL_DIM)
    ms = _dot_split(o * o, head_mean)
    y_b = o * lax.rsqrt(ms + RMS_EPS) * glanorm_ref[...] * (o_b * jax.nn.sigmoid(o_b))

    lane = lax.broadcasted_iota(jnp.int32, (1, POOL_WIDTH), 1)
    pos1 = n_past + t * rows + 1 + lax.broadcasted_iota(jnp.int32, (rows, 1), 0)
    d_parts = []
    for s in range(seqs):
        ext_s[s, POOL_HIST_ROWS:, :] = u_c[s * rows:(s + 1) * rows]
        e = ext_s[s]
        sums = []
        span = 1
        for w in POOL_WINDOWS:
            while span < w:
                e = e + pltpu.roll(e, span, 0)
                span *= 2
            sums.append(e[POOL_HIST_ROWS:])
        wsum = sums[-1]
        cnt = jnp.minimum(POOL_WINDOWS[-1], pos1)
        for gi in range(len(POOL_WINDOWS) - 2, -1, -1):
            in_group = lane < (gi + 1) * POOL_GROUP_DIM
            wsum = jnp.where(in_group, sums[gi], wsum)
            cnt = jnp.where(in_group, jnp.minimum(POOL_WINDOWS[gi], pos1), cnt)
        d_parts.append(wsum / cnt.astype(F32) - u_c[s * rows:(s + 1) * rows])
        ext_s[s, 0:POOL_HIST_ROWS, :] = ext_s[s, rows:rows + POOL_HIST_ROWS, :]
    d = jnp.concatenate(d_parts, axis=0)
    y_c = _dot(d.astype(BF16), poolw_ref[...]) * poolscale_ref[...]

    branches = ((ya_ref[...].reshape(m, SB_WIDTH), wa_ref), (y_b.astype(BF16), wb_ref), (y_c.astype(BF16), wc_ref))
    merged = None
    for bi, (y, w_ref) in enumerate(branches):
        gate_b = jax.nn.sigmoid(_dot(h, wgate_ref[:, bi * D_MODEL:(bi + 1) * D_MODEL]))
        term = gate_b * _dot(y, w_ref[...])
        merged = term if merged is None else merged + term
    o_ref[...] = (x + _dot(merged.astype(BF16), wout_ref[...])).reshape(seqs, rows, D_MODEL)

    @pl.when(t == pl.num_programs(1) - 1)
    def _():
        st_ref[...] = st_s[...]
        hist_ref[...] = ext_s[:, 0:POOL_HIST_ROWS, :]


def _mixer(x, y_a, st0, hist0, w, n_past):
    b, t, _ = x.shape
    rows = min(MIXER_TILE, t)
    seqs = MIXER_TILE // rows
    assert t % rows == 0 and b % seqs == 0 and rows % GLA_CHUNK == 0
    tok = lambda bi, ti: (bi, ti, 0)
    per_seq = lambda bi, ti: (bi, 0, 0)
    kw, vw = GLA_KEY_WIDTH, GLA_VAL_WIDTH
    return pl.pallas_call(
        functools.partial(_mixer_kernel, seqs=seqs, rows=rows, n_past=n_past),
        grid=(b // seqs, t // rows),
        in_specs=[pl.BlockSpec((seqs, rows, D_MODEL), tok),
                  pl.BlockSpec((seqs, rows, SB_WIDTH), tok),
                  pl.BlockSpec((seqs, vw, kw), per_seq),
                  pl.BlockSpec((seqs, POOL_HIST_ROWS, POOL_WIDTH), per_seq),
                  _const_spec((1, D_MODEL)),
                  _const_spec((D_MODEL, 3 * kw + 3 * vw)),
                  _const_spec((D_MODEL, 3 * D_MODEL)),
                  _const_spec((kw, kw)),
                  _const_spec((1, kw)),
                  _const_spec((1, vw)),
                  _const_spec((POOL_WIDTH, POOL_WIDTH)),
                  _const_spec((1, POOL_WIDTH)),
                  _const_spec((SB_WIDTH, D_MODEL)),
                  _const_spec((vw, D_MODEL)),
                  _const_spec((POOL_WIDTH, D_MODEL)),
                  _const_spec((D_MODEL, D_MODEL))],
        out_specs=[pl.BlockSpec((seqs, rows, D_MODEL), tok),
                   pl.BlockSpec((seqs, vw, kw), per_seq),
                   pl.BlockSpec((seqs, POOL_HIST_ROWS, POOL_WIDTH), per_seq)],
        out_shape=[jax.ShapeDtypeStruct((b, t, D_MODEL), F32),
                   jax.ShapeDtypeStruct((b, vw, kw), F32),
                   jax.ShapeDtypeStruct((b, POOL_HIST_ROWS, POOL_WIDTH), F32)],
        scratch_shapes=[pltpu.VMEM((seqs, vw, kw), F32),
                        pltpu.VMEM((seqs, POOL_HIST_ROWS + rows, POOL_WIDTH), F32)],
        compiler_params=_params("parallel", "arbitrary"),
        name="mixer",
    )(x, y_a, st0, hist0, w["mix_norm"], w["w_proj"], w["w_gate"], w["gla_wg"], w["gla_bg"], w["gla_norm"],
      w["pool_w"], w["pool_scale"], w["w_branch_a"], w["w_branch_b"], w["w_branch_c"], w["w_out"])


def _layer_weights(i, p):
    row = lambda a: a[i].reshape(1, -1).astype(F32)
    bf = lambda a: a.astype(BF16)
    kw, vw = GLA_KEY_WIDTH, GLA_VAL_WIDTH
    w_in = p["w_in"][i]
    a_end = 3 * SB_WIDTH
    qkv_b = w_in[:, a_end:a_end + 2 * kw + vw]
    r0 = a_end + 2 * kw + vw
    r_b = jnp.pad(w_in[:, r0:r0 + GLA_GATE_RANK], ((0, 0), (0, kw - GLA_GATE_RANK)))
    o0 = r0 + GLA_GATE_RANK
    rest = w_in[:, o0:o0 + vw + POOL_WIDTH]
    g0 = o0 + vw + POOL_WIDTH
    pool_w = jnp.zeros((POOL_WIDTH, POOL_WIDTH), F32)
    for gi in range(len(POOL_WINDOWS)):
        lo = gi * POOL_GROUP_DIM
        pool_w = pool_w.at[lo:lo + POOL_GROUP_DIM, lo:lo + POOL_GROUP_DIM].set(p["pool_w"][i, gi])
    return dict(
        ffn1_norm=row(p["ffn1_norm"]), ffn1_wa=bf(p["ffn1_w_in"][i, :, :FFN_DIM]), ffn1_wb=bf(p["ffn1_w_in"][i, :, FFN_DIM:]),
        ffn1_wo=bf(p["ffn1_w_out"][i]),
        mix_norm=row(p["mix_norm"]), w_qkv=bf(w_in[:, :a_end]),
        w_proj=bf(jnp.concatenate([qkv_b, r_b, rest], axis=1)), w_gate=bf(w_in[:, g0:]),
        gla_wg=bf(jnp.pad(p["gla_w_gate"][i], ((0, kw - GLA_GATE_RANK), (0, 0)))),
        gla_bg=row(p["gla_b_gate"]), gla_norm=row(p["gla_norm"]),
        pool_w=bf(pool_w), pool_scale=row(p["pool_scale"]),
        w_branch_a=bf(p["w_branch_a"][i]), w_branch_b=bf(p["w_branch_b"][i]), w_branch_c=bf(p["w_branch_c"][i]),
        w_out=bf(p["w_out"][i]),
        ffn2_norm=row(p["ffn2_norm"]), ffn2_wa=bf(p["ffn2_w_in"][i, :, :FFN_DIM]), ffn2_wb=bf(p["ffn2_w_in"][i, :, FFN_DIM:]),
        ffn2_wo=bf(p["ffn2_w_out"][i]),
        ple_norm=row(p["ple_norm"]), ple_wg=bf(p["ple_w_gate"][i]), ple_wp=bf(p["ple_w_proj"][i]),
    )


def _state_to_blocks(s):
    b = s.shape[0]
    eye = jnp.eye(GLA_HEADS, dtype=s.dtype)
    blocks = jnp.einsum("bhde,hg->bhegd", s, eye)
    return blocks.reshape(b, GLA_VAL_WIDTH, GLA_KEY_WIDTH)


def _blocks_to_state(st):
    b = st.shape[0]
    blocks = st.reshape(b, GLA_HEADS, GLA_VAL_DIM, GLA_HEADS, GLA_KEY_DIM)
    diag = jnp.stack([blocks[:, h, :, h, :] for h in range(GLA_HEADS)], axis=1)
    return diag.transpose(0, 1, 3, 2)


def _run_group(x, p, k_cache, v_cache, s_gla, s_pool, layers, final_norm):
    b, t, _ = x.shape
    n = b * t
    depth = len(layers)
    has_cache = k_cache is not None
    n_past = k_cache.shape[2] if has_cache else 0
    if has_cache:
        k_cache, v_cache = (a.transpose(0, 1, 3, 4, 2).reshape(depth, b, SB_WIDTH, n_past) for a in (k_cache, v_cache))
    x = x.reshape(n, D_MODEL)
    k_all = v_all = None
    ss, ps = [], []
    for i, w in enumerate(layers):
        x, q, k, v, k_all, v_all = _ffn_qkv(x, w, k_all, v_all, i, depth)
        q, k3, v3 = (a.reshape(b, t, SB_WIDTH) for a in (q, k, v))
        if has_cache:
            st0 = _state_to_blocks(s_gla[i])
            hist0 = jnp.pad(s_pool[i], ((0, 0), (POOL_HIST_ROWS - POOL_HIST, 0), (0, 0)))
        else:
            st0 = jnp.zeros((b, GLA_VAL_WIDTH, GLA_KEY_WIDTH), F32)
            hist0 = jnp.zeros((b, POOL_HIST_ROWS, POOL_WIDTH), F32)
        y_a = _sb_attention(q, k3, v3, k_cache, v_cache, i)
        x3, st, hist = _mixer(x.reshape(b, t, D_MODEL), y_a, st0, hist0, w, n_past)
        x = _ffn_ple(x3.reshape(n, D_MODEL), p.reshape(depth, n, PLE_DIM), i, w, final_norm, final=(i == depth - 1))
        ss.append(_blocks_to_state(st))
        ps.append(hist[:, POOL_HIST_ROWS - POOL_HIST:, :])
    heads = (depth, b, t, SB_HEADS, SB_HEAD_DIM)
    return x.reshape(b, t, D_MODEL), k_all.reshape(heads), v_all.reshape(heads), jnp.stack(ss), jnp.stack(ps)


def kernel(x_prompt, x_sample, cache_sb_k, cache_sb_v, state_gla, state_pool, p_prompt, p_sample, ffn1_norm, ffn1_w_in, ffn1_w_out, mix_norm, w_in, gla_w_gate, gla_b_gate, gla_norm, pool_w, pool_scale, w_branch_a, w_branch_b, w_branch_c, w_out, ffn2_norm, ffn2_w_in, ffn2_w_out, ple_norm, ple_w_gate, ple_w_proj, final_norm):
    params = dict(ffn1_norm=ffn1_norm, ffn1_w_in=ffn1_w_in, ffn1_w_out=ffn1_w_out, mix_norm=mix_norm, w_in=w_in,
                  gla_w_gate=gla_w_gate, gla_b_gate=gla_b_gate, gla_norm=gla_norm, pool_w=pool_w, pool_scale=pool_scale,
                  w_branch_a=w_branch_a, w_branch_b=w_branch_b, w_branch_c=w_branch_c, w_out=w_out,
                  ffn2_norm=ffn2_norm, ffn2_w_in=ffn2_w_in, ffn2_w_out=ffn2_w_out, ple_norm=ple_norm,
                  ple_w_gate=ple_w_gate, ple_w_proj=ple_w_proj)
    depth = w_in.shape[0]
    layers = [_layer_weights(i, params) for i in range(depth)]
    fnorm = final_norm.reshape(1, D_MODEL).astype(F32)
    prompt = _run_group(x_prompt, p_prompt, None, None, None, None, layers, fnorm)
    sample = _run_group(x_sample, p_sample, cache_sb_k, cache_sb_v, state_gla, state_pool, layers, fnorm)
    return (prompt[0], sample[0], prompt[1], prompt[2], prompt[3], prompt[4],
            sample[1], sample[2], sample[3], sample[4])
```
